```python
import jax, jax.numpy as jnp
from jax import lax
import numpy as np

D_MODEL = 1024
BATCH = 8
SEQ = 2048
DEPTH = 4

ATT_HEAD_DIM = 64
ATT_HEADS_PER_GROUP = D_MODEL // 256
DILATED_GROUPS = ((128, 1), (512, 4), (2048, 16))
N_GROUPS = len(DILATED_GROUPS)
ATT_WIDTH = N_GROUPS * ATT_HEADS_PER_GROUP * ATT_HEAD_DIM
ATT_OUT_WIDTH = ATT_HEADS_PER_GROUP * ATT_HEAD_DIM
ATT_BLOCK = 128

DN_HEAD_DIM = 128
DN_HEADS = D_MODEL // DN_HEAD_DIM
DN_WIDTH = DN_HEADS * DN_HEAD_DIM
CONV_WIDTH = 4
DN_CHUNK = 64

D_FF = 2816
EPS = 1e-6
N_ADA = 9

OFF_DN_QKV = 3 * ATT_WIDTH
OFF_DN_GATE = OFF_DN_QKV + 3 * DN_WIDTH
OFF_DN_A = OFF_DN_GATE + DN_WIDTH
OFF_DN_B = OFF_DN_A + DN_HEADS
OFF_MERGE = OFF_DN_B + DN_HEADS
N_IN = OFF_MERGE + 2 * D_MODEL

kernel_name = "hybrid_dilated_attn_gated_deltanet_macaron_adaln"


def rms_norm(x, g):
    xf = x.astype(jnp.float32)
    y = xf * lax.rsqrt(jnp.mean(xf * xf, axis=-1, keepdims=True) + EPS)
    return (y * g.astype(jnp.float32)).astype(x.dtype)


def l2_norm(x):
    xf = x.astype(jnp.float32)
    return xf * lax.rsqrt(jnp.sum(xf * xf, axis=-1, keepdims=True) + EPS)


def modulate(h, shift, scale):
    return h * (1.0 + scale[:, None, :]) + shift[:, None, :]


def swiglu(h, w_up, w_down):
    gate, up = jnp.split(h @ w_up, 2, axis=-1)
    return (jax.nn.silu(gate) * up) @ w_down


def dilated_group_attention(q, k, v, window, dilation):
    B, T, H, hd = q.shape
    L = T // dilation
    w_sub = window // dilation
    Lp = -(-L // ATT_BLOCK) * ATT_BLOCK
    nb = Lp // ATT_BLOCK

    def to_blocks(t):
        t = t.reshape(B, L, dilation, H, hd).transpose(0, 2, 1, 3, 4)
        t = jnp.pad(t, ((0, 0), (0, 0), (0, Lp - L), (0, 0), (0, 0)))
        return t.reshape(B, dilation, nb, ATT_BLOCK, H, hd)

    qb, kb, vb = to_blocks(q), to_blocks(k), to_blocks(v)

    def with_prev(t):
        prev = jnp.pad(t, ((0, 0), (0, 0), (1, 0), (0, 0), (0, 0), (0, 0)))[:, :, :-1]
        return jnp.concatenate([prev, t], axis=3)

    kk, vv = with_prev(kb), with_prev(vb)
    s = jnp.einsum('bgnqhd,bgnkhd->bgnhqk', qb, kk,
                   preferred_element_type=jnp.float32) * (ATT_HEAD_DIM ** -0.5)
    qi = jnp.arange(ATT_BLOCK)[:, None]
    kj = jnp.arange(2 * ATT_BLOCK)[None, :]
    dist = ATT_BLOCK + qi - kj
    blk = jnp.arange(nb)[:, None, None]
    valid = (dist >= 0) & (dist <= w_sub) & ((blk > 0) | (kj >= ATT_BLOCK))
    s = jnp.where(valid[None, None, :, None], s, -jnp.inf)
    m = jnp.max(s, axis=-1, keepdims=True)
    p = jnp.exp(s - m)
    denom = jnp.sum(p, axis=-1, keepdims=True)
    o = jnp.einsum('bgnhqk,bgnkhd->bgnqhd', p / denom, vv.astype(jnp.float32))
    lse = (m + jnp.log(denom))[..., 0].transpose(0, 1, 2, 4, 3)
    o = o.reshape(B, dilation, Lp, H, hd)[:, :, :L].transpose(0, 2, 1, 3, 4).reshape(B, T, H, hd)
    lse = lse.reshape(B, dilation, Lp, H)[:, :, :L].transpose(0, 2, 1, 3).reshape(B, T, H)
    return o, lse


def causal_depthwise_conv(x, w):
    T = x.shape[1]
    xp = jnp.pad(x, ((0, 0), (CONV_WIDTH - 1, 0), (0, 0)))
    return sum(xp[:, i:i + T] * w[i] for i in range(CONV_WIDTH))


def chunk_gated_delta_rule(q, k, v, g, beta):
    B, T, H, dk = q.shape
    dv = v.shape[-1]
    C = DN_CHUNK
    N = T // C

    def chunks(t):
        t = t.reshape(B, N, C, H, *t.shape[3:])
        return jnp.moveaxis(t, 3, 2)

    q, k, v, g, beta = chunks(q), chunks(k), chunks(v), chunks(g), chunks(beta)
    gc = jnp.cumsum(g, axis=-1)
    tril = jnp.tril(jnp.ones((C, C), dtype=bool))
    strict = tril & ~jnp.eye(C, dtype=bool)
    diff = gc[..., :, None] - gc[..., None, :]
    ldec = jnp.where(tril, jnp.exp(jnp.where(tril, diff, 0.0)), 0.0)
    kb = k * beta[..., None]
    vb = v * beta[..., None]
    a_mat = jnp.where(strict, jnp.einsum('bnhid,bnhjd->bnhij', kb, k) * ldec, 0.0)
    eye = jnp.broadcast_to(jnp.eye(C, dtype=jnp.float32), a_mat.shape)
    t_inv = lax.linalg.triangular_solve(eye + a_mat, eye, left_side=True, lower=True)
    u = jnp.einsum('bnhij,bnhjd->bnhid', t_inv, vb)
    w = jnp.einsum('bnhij,bnhjd->bnhid', t_inv, kb * jnp.exp(gc)[..., None])
    attn_intra = jnp.where(tril, jnp.einsum('bnhid,bnhjd->bnhij', q, k) * ldec, 0.0)
    q_dec = q * jnp.exp(gc)[..., None]
    k_dec = k * jnp.exp(gc[..., -1:] - gc)[..., None]
    g_last = jnp.exp(gc[..., -1])

    def step(S, inp):
        w_c, u_c, qd_c, kd_c, a_c, gl_c = inp
        v_new = u_c - jnp.einsum('bhcd,bhde->bhce', w_c, S)
        o = jnp.einsum('bhcd,bhde->bhce', qd_c, S) + jnp.einsum('bhij,bhje->bhie', a_c, v_new)
        S = S * gl_c[..., None, None] + jnp.einsum('bhcd,bhce->bhde', kd_c, v_new)
        return S, o

    xs = tuple(jnp.moveaxis(t, 1, 0) for t in (w, u, q_dec, k_dec, attn_intra, g_last))
    S0 = jnp.zeros((B, H, dk, dv), jnp.float32)
    _, o = lax.scan(step, S0, xs)
    return o.transpose(1, 0, 3, 2, 4).reshape(B, T, H, dv)


def hybrid_mixer(h, w_in, q_norm, k_norm, conv_w, a_log, dt_bias, dn_norm,
                 w_proj_att, w_proj_dn, w_out):
    B, T, _ = h.shape
    z = h @ w_in
    qkv = z[..., :OFF_DN_QKV].reshape(B, T, 3, N_GROUPS, ATT_HEADS_PER_GROUP, ATT_HEAD_DIM)
    q = rms_norm(qkv[:, :, 0], q_norm)
    k = rms_norm(qkv[:, :, 1], k_norm)
    v = qkv[:, :, 2]
    outs, lses = [], []
    for gi, (window, dilation) in enumerate(DILATED_GROUPS):
        o, lse = dilated_group_attention(q[:, :, gi], k[:, :, gi], v[:, :, gi], window, dilation)
        outs.append(o)
        lses.append(lse)
    wts = jax.nn.softmax(jnp.stack(lses), axis=0)
    y_att = jnp.sum(wts[..., None] * jnp.stack(outs), axis=0)
    y_att = y_att.reshape(B, T, ATT_OUT_WIDTH).astype(h.dtype) @ w_proj_att
    dn_qkv = jax.nn.silu(causal_depthwise_conv(z[..., OFF_DN_QKV:OFF_DN_GATE], conv_w))
    dq, dk, dv = jnp.split(dn_qkv, 3, axis=-1)
    dq = l2_norm(dq.reshape(B, T, DN_HEADS, DN_HEAD_DIM)) * (DN_HEAD_DIM ** -0.5)
    dk = l2_norm(dk.reshape(B, T, DN_HEADS, DN_HEAD_DIM))
    dv = dv.reshape(B, T, DN_HEADS, DN_HEAD_DIM).astype(jnp.float32)
    a_in = z[..., OFF_DN_A:OFF_DN_B].astype(jnp.float32)
    b_in = z[..., OFF_DN_B:OFF_MERGE].astype(jnp.float32)
    g_log = -jnp.exp(a_log.astype(jnp.float32)) * jax.nn.softplus(a_in + dt_bias.astype(jnp.float32))
    beta = jax.nn.sigmoid(b_in)
    o_dn = chunk_gated_delta_rule(dq, dk, dv, g_log, beta)
    out_gate = z[..., OFF_DN_GATE:OFF_DN_A].reshape(B, T, DN_HEADS, DN_HEAD_DIM).astype(jnp.float32)
    o_dn = rms_norm(o_dn, dn_norm) * jax.nn.silu(out_gate)
    y_dn = o_dn.reshape(B, T, DN_WIDTH).astype(h.dtype) @ w_proj_dn
    g_att, g_dn = jnp.split(jax.nn.sigmoid(z[..., OFF_MERGE:]), 2, axis=-1)
    return (g_att * y_att + g_dn * y_dn) @ w_out


def setup_inputs(seed: int = 0) -> dict:
    key = jax.random.key(seed)
    ks = iter(jax.random.split(key, 32))

    def nrm(shape, scale):
        return jax.random.normal(next(ks), shape, jnp.float32) * scale

    def gain(shape):
        return 1.0 + nrm(shape, 0.05)

    L, D = DEPTH, D_MODEL
    x = nrm((BATCH, SEQ, D), 1.0)
    c = nrm((BATCH, D), 1.0)
    ada_w = nrm((L, D, N_ADA * D), 0.02)
    ada_b = nrm((L, N_ADA * D), 0.1)
    norm_ff1 = gain((L, D))
    ffn1_w_up = nrm((L, D, 2 * D_FF), D ** -0.5)
    ffn1_w_down = nrm((L, D_FF, D), D_FF ** -0.5)
    norm_mix = gain((L, D))
    w_in = nrm((L, D, N_IN), D ** -0.5)
    q_norm = gain((L, ATT_HEAD_DIM))
    k_norm = gain((L, ATT_HEAD_DIM))
    conv_w = nrm((L, CONV_WIDTH, 3 * DN_WIDTH), CONV_WIDTH ** -0.5)
    a_log = jnp.log(jax.random.uniform(next(ks), (L, DN_HEADS), jnp.float32, 1.0, 16.0))
    dt = jnp.exp(jax.random.uniform(next(ks), (L, DN_HEADS), jnp.float32,
                                    float(np.log(1e-3)), float(np.log(1e-1))))
    dt_bias = jnp.log(jnp.expm1(dt))
    dn_norm = gain((L, DN_HEAD_DIM))
    w_proj_att = nrm((L, ATT_OUT_WIDTH, D), ATT_OUT_WIDTH ** -0.5)
    w_proj_dn = nrm((L, DN_WIDTH, D), DN_WIDTH ** -0.5)
    w_out = nrm((L, D, D), D ** -0.5)
    norm_ff2 = gain((L, D))
    ffn2_w_up = nrm((L, D, 2 * D_FF), D ** -0.5)
    ffn2_w_down = nrm((L, D_FF, D), D_FF ** -0.5)
    return {"x": x, "c": c, "ada_w": ada_w, "ada_b": ada_b,
            "norm_ff1": norm_ff1, "ffn1_w_up": ffn1_w_up, "ffn1_w_down": ffn1_w_down,
            "norm_mix": norm_mix, "w_in": w_in, "q_norm": q_norm, "k_norm": k_norm,
            "conv_w": conv_w, "a_log": a_log, "dt_bias": dt_bias, "dn_norm": dn_norm,
            "w_proj_att": w_proj_att, "w_proj_dn": w_proj_dn, "w_out": w_out,
            "norm_ff2": norm_ff2, "ffn2_w_up": ffn2_w_up, "ffn2_w_down": ffn2_w_down}


def reference(x, c, ada_w, ada_b, norm_ff1, ffn1_w_up, ffn1_w_down, norm_mix, w_in,
              q_norm, k_norm, conv_w, a_log, dt_bias, dn_norm, w_proj_att, w_proj_dn,
              w_out, norm_ff2, ffn2_w_up, ffn2_w_down):
    c_act = jax.nn.silu(c)
    for l in range(DEPTH):
        mod = c_act @ ada_w[l] + ada_b[l]
        (sh1, sc1, gt1, sh2, sc2, gt2, sh3, sc3, gt3) = jnp.split(mod, N_ADA, axis=-1)
        h = modulate(rms_norm(x, norm_ff1[l]), sh1, sc1)
        x = x + 0.5 * gt1[:, None, :] * swiglu(h, ffn1_w_up[l], ffn1_w_down[l])
        h = modulate(rms_norm(x, norm_mix[l]), sh2, sc2)
        x = x + gt2[:, None, :] * hybrid_mixer(h, w_in[l], q_norm[l], k_norm[l], conv_w[l],
                                                a_log[l], dt_bias[l], dn_norm[l],
                                                w_proj_att[l], w_proj_dn[l], w_out[l])
        h = modulate(rms_norm(x, norm_ff2[l]), sh3, sc3)
        x = x + 0.5 * gt3[:, None, :] * swiglu(h, ffn2_w_up[l], ffn2_w_down[l])
    return x
```

```python
import functools

import jax
import jax.numpy as jnp
from jax import lax
from jax.experimental import pallas as pl
from jax.experimental.pallas import tpu as pltpu

F32 = jnp.float32
BF16 = jnp.bfloat16

D_MODEL = 1024
DEPTH = 4
SEQ = 2048
D_FF = 2816
EPS = 1e-6
N_ADA = 9

ATT_HEAD_DIM = 64
ATT_HEADS = 4
ATT_GROUP_W = ATT_HEADS * ATT_HEAD_DIM
DILATIONS = (1, 4, 16)
ATT_BLOCK = 128
N_GROUPS = 3
ATT_W = N_GROUPS * ATT_GROUP_W

DN_HEADS = 8
DN_HEAD_DIM = 128
DN_W = DN_HEADS * DN_HEAD_DIM
DN_CHUNK = 64
CONV_W = 4

OFF_DN_QKV = 3 * ATT_W
OFF_DN_GATE = OFF_DN_QKV + 3 * DN_W
OFF_DN_A = OFF_DN_GATE + DN_W
OFF_DN_B = OFF_DN_A + DN_HEADS
OFF_MERGE = OFF_DN_B + DN_HEADS
N_IN = OFF_MERGE + 2 * D_MODEL

P_ATT = 0
P_DN = P_ATT + 3 * ATT_W
P_GATE = P_DN + 3 * DN_W
P_MERGE = P_GATE + DN_W
P_AB = P_MERGE + 2 * D_MODEL
P_TOTAL = P_AB + 128

LANE = 128
SUBLANE = 8
NEG_BIG = -1e30
VMEM_LIMIT = 56 * 1024 * 1024


def _dot(a, b):
    return jnp.dot(a, b, preferred_element_type=F32)


def _dot_nt(a, b):
    return lax.dot_general(a, b, (((1,), (1,)), ((), ())), preferred_element_type=F32)


def _dot_exact(a, b):
    return jnp.dot(a, b, preferred_element_type=F32, precision=lax.Precision.HIGHEST)


def _sigmoid(x):
    return 1.0 / (1.0 + jnp.exp(-x))


def _silu(x):
    return x * _sigmoid(x)


def _norm_mod(x, gain, shift, scale):
    ms = jnp.mean(x * x, axis=-1, keepdims=True)
    y = (x * lax.rsqrt(ms + EPS)) * gain
    return y * (1.0 + scale) + shift


def _const_spec(shape, single_buffer=True):
    nd = len(shape)
    kwargs = {"pipeline_mode": pl.Buffered(1)} if single_buffer else {}
    return pl.BlockSpec(shape, lambda *_: (0,) * nd, **kwargs)


def _mod_spec(k):
    return pl.BlockSpec((1, 1, D_MODEL), lambda b, i: (b, 0, k))


def _ada_kernel(c_ref, w_ref, b_ref, o_ref):
    c = c_ref[...]
    ca = _silu(c).astype(BF16)
    o_ref[0] = _dot(ca, w_ref[0].astype(BF16)) + b_ref[0]


def _ada_mod(c, ada_w, ada_b, tn=1024):
    depth, d, n = ada_w.shape
    bsz = c.shape[0]
    return pl.pallas_call(
        _ada_kernel,
        grid=(depth, n // tn),
        in_specs=[
            pl.BlockSpec((bsz, d), lambda l, j: (0, 0)),
            pl.BlockSpec((1, d, tn), lambda l, j: (l, 0, j)),
            pl.BlockSpec((1, 1, tn), lambda l, j: (l, 0, j)),
        ],
        out_specs=pl.BlockSpec((1, bsz, tn), lambda l, j: (l, 0, j)),
        out_shape=jax.ShapeDtypeStruct((depth, bsz, n), F32),
        name="ada_mod",
    )(c, ada_w, ada_b.reshape(depth, 1, n))


def _ffn_kernel(x_ref, g_ref, sh_ref, sc_ref, gt_ref, wup_ref, wdn_ref, o_ref, *, tf):
    x = x_ref[0]
    hb = _norm_mod(x, g_ref[...], sh_ref[0], sc_ref[0]).astype(BF16)
    acc = jnp.zeros(x.shape, F32)
    for f in range(D_FF // tf):
        gate = _dot(hb, wup_ref[:, f * tf:(f + 1) * tf])
        up = _dot(hb, wup_ref[:, D_FF + f * tf:D_FF + (f + 1) * tf])
        a = (_silu(gate) * up).astype(BF16)
        acc = acc + _dot(a, wdn_ref[f * tf:(f + 1) * tf, :])
    o_ref[0] = x + (0.5 * gt_ref[0]) * acc


def _ffn(x, mod_l, k0, gain, wup, wdn, tm=512, tf=256):
    bsz, t, d = x.shape
    return pl.pallas_call(
        functools.partial(_ffn_kernel, tf=tf),
        grid=(bsz, t // tm),
        in_specs=[
            pl.BlockSpec((1, tm, d), lambda b, i: (b, i, 0)),
            _const_spec((1, d), single_buffer=False),
            _mod_spec(k0), _mod_spec(k0 + 1), _mod_spec(k0 + 2),
            _const_spec((d, 2 * D_FF)),
            _const_spec((D_FF, d)),
        ],
        out_specs=pl.BlockSpec((1, tm, d), lambda b, i: (b, i, 0)),
        out_shape=jax.ShapeDtypeStruct(x.shape, F32),
        compiler_params=pltpu.CompilerParams(
            dimension_semantics=("arbitrary", "arbitrary"), vmem_limit_bytes=VMEM_LIMIT),
        name="ffn",
    )(x, gain.reshape(1, d), mod_l, mod_l, mod_l, wup, wdn)


def _pack_w_in(w):
    q, k, v = w[:, :ATT_W], w[:, ATT_W:2 * ATT_W], w[:, 2 * ATT_W:3 * ATT_W]
    parts = []
    for g in range(N_GROUPS):
        sl = slice(g * ATT_GROUP_W, (g + 1) * ATT_GROUP_W)
        parts += [q[:, sl], k[:, sl], v[:, sl]]
    parts.append(w[:, OFF_DN_QKV:OFF_DN_A])
    parts.append(w[:, OFF_MERGE:])
    parts.append(w[:, OFF_DN_A:OFF_MERGE])
    parts.append(jnp.zeros((w.shape[0], 128 - 2 * DN_HEADS), w.dtype))
    return jnp.concatenate(parts, axis=1).astype(BF16)


def _inproj_kernel(x_ref, g_ref, sh_ref, sc_ref, w_ref, ones_ref, qn_ref, kn_ref, cw_ref,
                   za0_ref, za1_ref, za2_ref, dq_ref, dk_ref, dv_ref, gate_ref, mg_ref,
                   ab_ref, abt_ref, carry_ref, cscr_ref, *, tm, cn):
    x = x_ref[0]
    hb = _norm_mod(x, g_ref[...], sh_ref[0], sc_ref[0]).astype(BF16)

    ones = ones_ref[...]
    for g, za_ref in enumerate((za0_ref, za1_ref, za2_ref)):
        c0 = P_ATT + g * 3 * ATT_GROUP_W
        for part, gain_ref, mult in ((0, qn_ref, ATT_HEAD_DIM ** -0.5), (1, kn_ref, 1.0)):
            z = _dot(hb, w_ref[:, c0 + part * ATT_GROUP_W:c0 + (part + 1) * ATT_GROUP_W])
            ms = _dot((z * z).astype(BF16), ones)
            zn = (z * lax.rsqrt(ms + EPS)) * (gain_ref[...] * mult)
            za_ref[0, :, part * ATT_GROUP_W:(part + 1) * ATT_GROUP_W] = zn.astype(BF16)
        z = _dot(hb, w_ref[:, c0 + 2 * ATT_GROUP_W:c0 + 3 * ATT_GROUP_W])
        za_ref[0, :, 2 * ATT_GROUP_W:3 * ATT_GROUP_W] = z.astype(BF16)

    @pl.when(pl.program_id(1) == 0)
    def _():
        carry_ref[...] = jnp.zeros(carry_ref.shape, F32)

    for part, out_ref in enumerate((dq_ref, dk_ref, dv_ref)):
        for j in range(DN_W // cn):
            dc = part * DN_W + j * cn
            z = _dot(hb, w_ref[:, P_DN + dc:P_DN + dc + cn])
            cscr_ref[0:SUBLANE, :] = carry_ref[:, dc:dc + cn]
            cscr_ref[SUBLANE:SUBLANE + tm, :] = z
            carry_ref[:, dc:dc + cn] = z[tm - SUBLANE:tm, :]
            y = jnp.zeros((tm, cn), F32)
            for i in range(CONV_W):
                s0 = SUBLANE - (CONV_W - 1) + i
                y = y + cw_ref[i:i + 1, dc:dc + cn] * cscr_ref[s0:s0 + tm, :]
            y = _silu(y)
            if part < 2:
                mult = DN_HEAD_DIM ** -0.5 if part == 0 else 1.0
                for h in range(cn // DN_HEAD_DIM):
                    yh = y[:, h * DN_HEAD_DIM:(h + 1) * DN_HEAD_DIM]
                    ss = jnp.sum(yh * yh, axis=-1, keepdims=True)
                    yn = yh * (lax.rsqrt(ss + EPS) * mult)
                    lo = j * cn + h * DN_HEAD_DIM
                    out_ref[0, :, lo:lo + DN_HEAD_DIM] = yn.astype(BF16)
            else:
                out_ref[0, :, j * cn:(j + 1) * cn] = y.astype(BF16)

    for j in range(DN_W // cn):
        z = _dot(hb, w_ref[:, P_GATE + j * cn:P_GATE + (j + 1) * cn])
        gate_ref[0, :, j * cn:(j + 1) * cn] = z.astype(BF16)
    for j in range(2 * D_MODEL // cn):
        z = _dot(hb, w_ref[:, P_MERGE + j * cn:P_MERGE + (j + 1) * cn])
        mg_ref[0, :, j * cn:(j + 1) * cn] = _sigmoid(z).astype(BF16)

    z = _dot(hb, w_ref[:, P_AB:P_AB + 128])
    ab_ref[0] = z[:, :2 * DN_HEADS]
    abt_ref[0] = z.T[:2 * DN_HEADS, :]


def _inproj(x, mod_l, gain, w_packed, ones_blk, qn4, kn4, conv_w, tm=512, cn=512):
    bsz, t, d = x.shape
    bf = lambda n: jax.ShapeDtypeStruct((bsz, t, n), BF16)
    row = lambda n: pl.BlockSpec((1, tm, n), lambda b, i: (b, i, 0))
    return pl.pallas_call(
        functools.partial(_inproj_kernel, tm=tm, cn=cn),
        grid=(bsz, t // tm),
        in_specs=[
            row(d),
            _const_spec((1, d), single_buffer=False),
            _mod_spec(3), _mod_spec(4),
            _const_spec((d, P_TOTAL)),
            _const_spec((ATT_GROUP_W, ATT_GROUP_W)),
            _const_spec((1, ATT_GROUP_W), single_buffer=False),
            _const_spec((1, ATT_GROUP_W), single_buffer=False),
            _const_spec((CONV_W, 3 * DN_W), single_buffer=False),
        ],
        out_specs=[row(3 * ATT_GROUP_W)] * 3 + [row(DN_W)] * 4 + [row(2 * D_MODEL)]
        + [row(2 * DN_HEADS), pl.BlockSpec((1, 2 * DN_HEADS, tm), lambda b, i: (b, 0, i))],
        out_shape=[bf(3 * ATT_GROUP_W)] * 3 + [bf(DN_W)] * 4 + [bf(2 * D_MODEL)]
        + [jax.ShapeDtypeStruct((bsz, t, 2 * DN_HEADS), F32),
           jax.ShapeDtypeStruct((bsz, 2 * DN_HEADS, t), F32)],
        scratch_shapes=[pltpu.VMEM((SUBLANE, 3 * DN_W), F32),
                        pltpu.VMEM((SUBLANE + tm, cn), F32)],
        compiler_params=pltpu.CompilerParams(
            dimension_semantics=("arbitrary", "arbitrary"), vmem_limit_bytes=VMEM_LIMIT),
        name="inproj",
    )(x, gain.reshape(1, d), mod_l, mod_l, w_packed, ones_blk, qn4, kn4, conv_w)


def _attn_block(q, kk, vv, bias, head_masks):
    nq = q.shape[0]
    zero = jnp.zeros_like(q)
    qs = jnp.concatenate([jnp.where(hm, q, zero) for hm in head_masks], axis=0)
    s = _dot_nt(qs, kk) + jnp.concatenate([bias] * ATT_HEADS, axis=0)
    m = jnp.max(s, axis=-1, keepdims=True)
    p = jnp.exp(s - m)
    l = jnp.sum(p, axis=-1, keepdims=True)
    pv = _dot(p.astype(BF16), vv)
    inv = 1.0 / l
    lse = m + jnp.log(l)
    o = jnp.zeros((nq, ATT_GROUP_W), F32)
    ls = jnp.zeros((nq, ATT_GROUP_W), F32)
    for h, hm in enumerate(head_masks):
        rows = slice(h * nq, (h + 1) * nq)
        o = jnp.where(hm, pv[rows] * inv[rows], o)
        ls = jnp.where(hm, jnp.broadcast_to(lse[rows], (nq, ATT_GROUP_W)), ls)
    return o, ls


def _attn_kernel(q1_ref, k1_ref, v1_ref, k1p_ref, v1p_ref, q2_ref, k2_ref, v2_ref, z3_ref,
                 o1_ref, l1_ref, o2_ref, l2_ref, o3_ref, l3_ref):
    nb = ATT_BLOCK
    step = pl.program_id(1)
    lane = lax.broadcasted_iota(jnp.int32, (1, ATT_GROUP_W), 1)
    head_masks = [(lane // ATT_HEAD_DIM) == h for h in range(ATT_HEADS)]
    qi = lax.broadcasted_iota(jnp.int32, (nb, 2 * nb), 0)
    kj = lax.broadcasted_iota(jnp.int32, (nb, 2 * nb), 1)
    ok_two = (kj >= qi) & (kj <= qi + nb)
    ok_cur = (kj >= nb) & (kj <= qi + nb)
    bias_two = jnp.where(ok_two, 0.0, NEG_BIG).astype(F32)
    bias_first = jnp.where(ok_cur, 0.0, NEG_BIG).astype(F32)
    bias_one = bias_first[:, nb:]

    for i in range(4):
        rows = slice(i * nb, (i + 1) * nb)
        if i == 0:
            kk = jnp.concatenate([k1p_ref[0], k1_ref[0, rows]], axis=0)
            vv = jnp.concatenate([v1p_ref[0], v1_ref[0, rows]], axis=0)
            bias = jnp.where(step == 0, bias_first, bias_two)
        else:
            keys = slice((i - 1) * nb, (i + 1) * nb)
            kk, vv, bias = k1_ref[0, keys], v1_ref[0, keys], bias_two
        o, ls = _attn_block(q1_ref[0, rows], kk, vv, bias, head_masks)
        o1_ref[0, rows] = o
        l1_ref[0, rows] = ls

    for i in range(4):
        rows = slice(i * nb, (i + 1) * nb)
        if i == 0:
            kk, vv, bias = k2_ref[0, rows], v2_ref[0, rows], bias_one
        else:
            keys = slice((i - 1) * nb, (i + 1) * nb)
            kk, vv, bias = k2_ref[0, keys], v2_ref[0, keys], bias_two
        o, ls = _attn_block(q2_ref[0, rows], kk, vv, bias, head_masks)
        o2_ref[0, rows] = o
        l2_ref[0, rows] = ls

    for r in range(4):
        c0 = r * 3 * ATT_GROUP_W
        q = z3_ref[0, :, c0:c0 + ATT_GROUP_W]
        kk = z3_ref[0, :, c0 + ATT_GROUP_W:c0 + 2 * ATT_GROUP_W]
        vv = z3_ref[0, :, c0 + 2 * ATT_GROUP_W:c0 + 3 * ATT_GROUP_W]
        o, ls = _attn_block(q, kk, vv, bias_one, head_masks)
        o3_ref[0, :, r * ATT_GROUP_W:(r + 1) * ATT_GROUP_W] = o
        l3_ref[0, :, r * ATT_GROUP_W:(r + 1) * ATT_GROUP_W] = ls


def _attention(za0, za1, za2):
    bsz, t, _ = za0.shape
    gw, nb = ATT_GROUP_W, ATT_BLOCK
    steps = 4
    rows1 = t // steps
    assert rows1 == 4 * nb and t // DILATIONS[1] == 4 * nb and t // DILATIONS[2] == nb
    z2 = za1.reshape(bsz, t // DILATIONS[1], DILATIONS[1] * 3 * gw)
    z3 = za2.reshape(bsz, t // DILATIONS[2], DILATIONS[2] * 3 * gw)
    prev = lambda b, j: (b, jnp.maximum(4 * j - 1, 0), 1)
    prev_v = lambda b, j: (b, jnp.maximum(4 * j - 1, 0), 2)
    in_specs = [
        pl.BlockSpec((1, rows1, gw), lambda b, j: (b, j, 0)),
        pl.BlockSpec((1, rows1, gw), lambda b, j: (b, j, 1)),
        pl.BlockSpec((1, rows1, gw), lambda b, j: (b, j, 2)),
        pl.BlockSpec((1, nb, gw), prev),
        pl.BlockSpec((1, nb, gw), prev_v),
        pl.BlockSpec((1, 4 * nb, gw), lambda b, j: (b, 0, 3 * j)),
        pl.BlockSpec((1, 4 * nb, gw), lambda b, j: (b, 0, 3 * j + 1)),
        pl.BlockSpec((1, 4 * nb, gw), lambda b, j: (b, 0, 3 * j + 2)),
        pl.BlockSpec((1, nb, 4 * 3 * gw), lambda b, j: (b, 0, j)),
    ]
    o1s = pl.BlockSpec((1, rows1, gw), lambda b, j: (b, j, 0))
    o2s = pl.BlockSpec((1, 4 * nb, gw), lambda b, j: (b, 0, j))
    o3s = pl.BlockSpec((1, nb, 4 * gw), lambda b, j: (b, 0, j))
    s1 = jax.ShapeDtypeStruct((bsz, t, gw), F32)
    s2 = jax.ShapeDtypeStruct((bsz, t // DILATIONS[1], DILATIONS[1] * gw), F32)
    s3 = jax.ShapeDtypeStruct((bsz, t // DILATIONS[2], DILATIONS[2] * gw), F32)
    o1, l1, o2, l2, o3, l3 = pl.pallas_call(
        _attn_kernel,
        grid=(bsz, steps),
        in_specs=in_specs,
        out_specs=[o1s, o1s, o2s, o2s, o3s, o3s],
        out_shape=[s1, s1, s2, s2, s3, s3],
        compiler_params=pltpu.CompilerParams(
            dimension_semantics=("arbitrary", "arbitrary"), vmem_limit_bytes=VMEM_LIMIT),
        name="dilated_attn",
    )(za0, za0, za0, za0, za0, z2, z2, z2, z3)
    nat = lambda a: a.reshape(bsz, t, gw)
    return o1, l1, nat(o2), nat(l2), nat(o3), nat(l3)


def _softplus(x):
    return jnp.maximum(x, 0.0) + jnp.log(1.0 + jnp.exp(-jnp.abs(x)))


def _chunk_masks():
    c = DN_CHUNK
    ii = lax.broadcasted_iota(jnp.int32, (c, c), 0)
    jj = lax.broadcasted_iota(jnp.int32, (c, c), 1)
    return ii >= jj, ii > jj, ii <= jj


def _chunk_decay(ab, abt, alog_ref, dtb_ref, alog_c_ref, dtb_c_ref, tril, triu):
    g_col = -jnp.exp(alog_ref[...]) * _softplus(ab + dtb_ref[...])
    g_row = -jnp.exp(alog_c_ref[...]) * _softplus(abt + dtb_c_ref[...])
    gc_col = _dot_exact(jnp.where(tril, 1.0, 0.0).astype(F32), g_col)
    gc_row = _dot_exact(g_row, jnp.where(triu, 1.0, 0.0).astype(F32))
    return gc_col, gc_row, _sigmoid(ab)


def _head_decay(gc_col, gc_row, beta_col, h, tril):
    c, hd = DN_CHUNK, DN_HEAD_DIM
    cb = jnp.broadcast_to(gc_col[:, h:h + 1], (c, hd))
    bb = jnp.broadcast_to(beta_col[:, DN_HEADS + h:DN_HEADS + h + 1], (c, hd))
    rb = jnp.broadcast_to(gc_row[h:h + 1, :], (c, c))
    ld = jnp.exp(jnp.where(tril, cb[:, :c] - rb, 0.0))
    return cb, bb, ld


def _dn_prep_kernel(k_ref, ab_ref, abt_ref, alog_ref, dtb_ref, alog_c_ref, dtb_c_ref, a_ref,
                    *, n_chunks):
    c, hd = DN_CHUNK, DN_HEAD_DIM
    tril, strict, triu = _chunk_masks()
    for n in range(n_chunks):
        r0 = n * c
        gc_col, gc_row, beta_col = _chunk_decay(ab_ref[0, r0:r0 + c, :], abt_ref[0, n], alog_ref,
                                                dtb_ref, alog_c_ref, dtb_c_ref, tril, triu)
        for h in range(DN_HEADS):
            k = k_ref[0, r0:r0 + c, h * hd:(h + 1) * hd]
            _, bb, ld = _head_decay(gc_col, gc_row, beta_col, h, tril)
            a_ref[0, n, h] = jnp.where(strict, _dot_nt(k, k) * (bb[:, :c] * ld), 0.0)


def _dn_solve_kernel(a_ref, t_ref, at_scr, tt_scr):
    c = DN_CHUNK
    tiles = c * c // LANE
    per = LANE // c
    for t in range(tiles):
        blk = a_ref[:, t * LANE:(t + 1) * LANE].T
        for r in range(per):
            at_scr[t * per + r] = blk[r * c:(r + 1) * c]
    row_id = lax.broadcasted_iota(jnp.int32, (SUBLANE, LANE), 0)
    zero = jnp.zeros((SUBLANE, LANE), F32)
    for i in range(c):
        groups = i // SUBLANE + 1
        acc = [zero] * (c // SUBLANE)
        acc[groups - 1] = jnp.where(row_id == i % SUBLANE, 1.0, 0.0).astype(F32)
        for j in range(i):
            a = jnp.broadcast_to(at_scr[i, j:j + 1, :], (SUBLANE, LANE))
            for g in range(j // SUBLANE + 1):
                acc[g] = acc[g] - a * tt_scr[j, g * SUBLANE:(g + 1) * SUBLANE, :]
        for g in range(c // SUBLANE):
            tt_scr[i, g * SUBLANE:(g + 1) * SUBLANE, :] = acc[g]
    for t in range(tiles):
        blk = jnp.concatenate([tt_scr[t * per + r] for r in range(per)], axis=0)
        t_ref[:, t * LANE:(t + 1) * LANE] = blk.T


def _dn_scan_kernel(q_ref, k_ref, v_ref, gate_ref, ab_ref, abt_ref, t_ref, alog_ref, dtb_ref,
                    alog_c_ref, dtb_c_ref, nrm_ref, o_ref, s_ref, *, n_chunks):
    c, hd = DN_CHUNK, DN_HEAD_DIM

    @pl.when(pl.program_id(1) == 0)
    def _():
        s_ref[...] = jnp.zeros(s_ref.shape, F32)

    tril, _, triu = _chunk_masks()
    for n in range(n_chunks):
        r0 = n * c
        gc_col, gc_row, beta_col = _chunk_decay(ab_ref[0, r0:r0 + c, :], abt_ref[0, n], alog_ref,
                                                dtb_ref, alog_c_ref, dtb_c_ref, tril, triu)
        for h in range(DN_HEADS):
            lanes = slice(h * hd, (h + 1) * hd)
            qb = q_ref[0, r0:r0 + c, lanes]
            kb16 = k_ref[0, r0:r0 + c, lanes]
            q, k = qb.astype(F32), kb16.astype(F32)
            v = v_ref[0, r0:r0 + c, lanes].astype(F32)
            cb, bb, ld = _head_decay(gc_col, gc_row, beta_col, h, tril)
            eb = jnp.exp(cb)
            attn = jnp.where(tril, _dot_nt(qb, kb16) * ld, 0.0)
            kbeta = k * bb
            rhs = jnp.concatenate([v * bb, kbeta * eb], axis=1).astype(BF16)
            x = _dot(t_ref[0, n, h].astype(BF16), rhs)
            u, w = x[:, :hd], x[:, hd:]
            s = s_ref[h]
            ws = _dot(jnp.concatenate([w, q * eb], axis=0).astype(BF16), s.astype(BF16))
            vn = (u - ws[:c]).astype(BF16)
            gl = cb[c - 1:c, :]
            kdec = k * jnp.exp(gl - cb)
            o = ws[c:] + _dot(attn.astype(BF16), vn)
            s_ref[h] = s * jnp.exp(gl) + _dot(kdec.T.astype(BF16), vn)
            ms = jnp.mean(o * o, axis=-1, keepdims=True)
            on = (o * lax.rsqrt(ms + EPS)) * nrm_ref[...]
            gate = gate_ref[0, r0:r0 + c, lanes].astype(F32)
            o_ref[0, r0:r0 + c, lanes] = (on * _silu(gate)).astype(BF16)


def _deltanet(dq, dk, dv, gate, ab, abt, a_log, dt_bias, dn_norm, rows=128):
    bsz, t, w = dq.shape
    c = DN_CHUNK
    n_chunks = rows // c
    total_chunks = t // c
    abt_c = abt.reshape(bsz, 2 * DN_HEADS, total_chunks, c).transpose(0, 2, 1, 3)
    pad = jnp.zeros((DN_HEADS,), F32)
    alog16 = jnp.concatenate([a_log, pad])
    dtb16 = jnp.concatenate([dt_bias, pad])
    small = [alog16.reshape(1, -1), dtb16.reshape(1, -1), alog16.reshape(-1, 1), dtb16.reshape(-1, 1)]
    small_specs = [_const_spec((1, 2 * DN_HEADS), single_buffer=False)] * 2 + \
                  [_const_spec((2 * DN_HEADS, 1), single_buffer=False)] * 2
    row = pl.BlockSpec((1, rows, w), lambda b, i: (b, i, 0))
    ab_spec = pl.BlockSpec((1, rows, 2 * DN_HEADS), lambda b, i: (b, i, 0))
    abt_spec = pl.BlockSpec((1, n_chunks, 2 * DN_HEADS, c), lambda b, i: (b, i, 0, 0))
    mat_spec = pl.BlockSpec((1, n_chunks, DN_HEADS, c, c), lambda b, i: (b, i, 0, 0, 0))
    mat_shape = jax.ShapeDtypeStruct((bsz, total_chunks, DN_HEADS, c, c), F32)
    params = pltpu.CompilerParams(dimension_semantics=("arbitrary", "arbitrary"),
                                  vmem_limit_bytes=VMEM_LIMIT)

    a_mat = pl.pallas_call(
        functools.partial(_dn_prep_kernel, n_chunks=n_chunks),
        grid=(bsz, t // rows),
        in_specs=[row, ab_spec, abt_spec] + small_specs,
        out_specs=mat_spec,
        out_shape=mat_shape,
        compiler_params=params,
        name="dn_prep",
    )(dk, ab, abt_c, *small)

    n_sys = bsz * total_chunks * DN_HEADS
    assert n_sys % LANE == 0
    flat_spec = pl.BlockSpec((LANE, c * c), lambda i: (i, 0))
    t_mat = pl.pallas_call(
        _dn_solve_kernel,
        grid=(n_sys // LANE,),
        in_specs=[flat_spec],
        out_specs=flat_spec,
        out_shape=jax.ShapeDtypeStruct((n_sys, c * c), F32),
        scratch_shapes=[pltpu.VMEM((c, c, LANE), F32), pltpu.VMEM((c, c, LANE), F32)],
        compiler_params=pltpu.CompilerParams(dimension_semantics=("arbitrary",),
                                             vmem_limit_bytes=VMEM_LIMIT),
        name="dn_solve",
    )(a_mat.reshape(n_sys, c * c)).reshape(mat_shape.shape)

    return pl.pallas_call(
        functools.partial(_dn_scan_kernel, n_chunks=n_chunks),
        grid=(bsz, t // rows),
        in_specs=[row, row, row, row, ab_spec, abt_spec, mat_spec] + small_specs
        + [_const_spec((1, DN_HEAD_DIM), single_buffer=False)],
        out_specs=row,
        out_shape=jax.ShapeDtypeStruct((bsz, t, w), BF16),
        scratch_shapes=[pltpu.VMEM((DN_HEADS, DN_HEAD_DIM, DN_HEAD_DIM), F32)],
        compiler_params=params,
        name="dn_scan",
    )(dq, dk, dv, gate, ab, abt_c, t_mat, *small, dn_norm.reshape(1, -1))


def _merge_kernel(o1_ref, l1_ref, o2_ref, l2_ref, o3_ref, l3_ref, odn_ref, mg_ref, x_ref, gt_ref,
                  wpa_ref, wpd_ref, wo_ref, out_ref):
    l1, l2, l3 = l1_ref[0], l2_ref[0], l3_ref[0]
    m = jnp.maximum(jnp.maximum(l1, l2), l3)
    e1, e2, e3 = jnp.exp(l1 - m), jnp.exp(l2 - m), jnp.exp(l3 - m)
    y = (e1 * o1_ref[0] + e2 * o2_ref[0] + e3 * o3_ref[0]) / (e1 + e2 + e3)
    ya = _dot(y.astype(BF16), wpa_ref[...])
    yd = _dot(odn_ref[0], wpd_ref[...])
    d = D_MODEL
    mix = mg_ref[0, :, :d].astype(F32) * ya + mg_ref[0, :, d:].astype(F32) * yd
    out = _dot(mix.astype(BF16), wo_ref[...])
    out_ref[0] = x_ref[0] + gt_ref[0] * out


def _merge(att, odn, mg, x, mod_l, wpa, wpd, wo, tm=512):
    bsz, t, d = x.shape
    row = lambda n: pl.BlockSpec((1, tm, n), lambda b, i: (b, i, 0))
    return pl.pallas_call(
        _merge_kernel,
        grid=(bsz, t // tm),
        in_specs=[row(ATT_GROUP_W)] * 6 + [row(DN_W), row(2 * d), row(d), _mod_spec(5),
                                            _const_spec(wpa.shape), _const_spec(wpd.shape),
                                            _const_spec(wo.shape)],
        out_specs=row(d),
        out_shape=jax.ShapeDtypeStruct(x.shape, F32),
        compiler_params=pltpu.CompilerParams(
            dimension_semantics=("arbitrary", "arbitrary"), vmem_limit_bytes=VMEM_LIMIT),
        name="merge_out",
    )(*att, odn, mg, x, mod_l, wpa, wpd, wo)


def _mixer(x, mod_l, gain, w_packed, ones_blk, qn4, kn4, conv_w, a_log, dt_bias, dn_norm,
           wpa, wpd, wo):
    za0, za1, za2, dq, dk, dv, gate, mg, ab, abt = _inproj(
        x, mod_l, gain, w_packed, ones_blk, qn4, kn4, conv_w)
    att = _attention(za0, za1, za2)
    odn = _deltanet(dq, dk, dv, gate, ab, abt, a_log, dt_bias, dn_norm)
    return _merge(att, odn, mg, x, mod_l, wpa, wpd, wo)


def kernel(x, c, ada_w, ada_b, norm_ff1, ffn1_w_up, ffn1_w_down, norm_mix, w_in, q_norm, k_norm,
           conv_w, a_log, dt_bias, dn_norm, w_proj_att, w_proj_dn, w_out, norm_ff2, ffn2_w_up,
           ffn2_w_down):
    bsz = x.shape[0]
    mod = _ada_mod(c, ada_w, ada_b)
    blk = jnp.arange(ATT_GROUP_W) // ATT_HEAD_DIM
    ones_blk = jnp.where(blk[:, None] == blk[None, :], 1.0 / ATT_HEAD_DIM, 0.0).astype(BF16)
    for l in range(DEPTH):
        mod_l = mod[l].reshape(bsz, 1, N_ADA * D_MODEL)
        x = _ffn(x, mod_l, 0, norm_ff1[l], ffn1_w_up[l].astype(BF16), ffn1_w_down[l].astype(BF16))
        x = _mixer(x, mod_l, norm_mix[l], _pack_w_in(w_in[l]), ones_blk,
                   jnp.tile(q_norm[l], ATT_HEADS).reshape(1, -1),
                   jnp.tile(k_norm[l], ATT_HEADS).reshape(1, -1),
                   conv_w[l], a_log[l], dt_bias[l], dn_norm[l],
                   w_proj_att[l].astype(BF16), w_proj_dn[l].astype(BF16), w_out[l].astype(BF16))
        x = _ffn(x, mod_l, 6, norm_ff2[l], ffn2_w_up[l].astype(BF16), ffn2_w_down[l].astype(BF16))
    return x
```

```python
import functools

import jax
import jax.numpy as jnp
from jax import lax
from jax.experimental import pallas as pl
from jax.experimental.pallas import tpu as pltpu

F32 = jnp.float32
BF16 = jnp.bfloat16

D_MODEL = 1024
DEPTH = 4
SEQ = 2048
D_FF = 2816
EPS = 1e-6
N_ADA = 9

ATT_HEAD_DIM = 64
ATT_HEADS = 4
ATT_GROUP_W = ATT_HEADS * ATT_HEAD_DIM
DILATIONS = (1, 4, 16)
ATT_BLOCK = 128
N_GROUPS = 3
ATT_W = N_GROUPS * ATT_GROUP_W

DN_HEADS = 8
DN_HEAD_DIM = 128
DN_W = DN_HEADS * DN_HEAD_DIM
DN_CHUNK = 64
CONV_W = 4

OFF_DN_QKV = 3 * ATT_W
OFF_DN_GATE = OFF_DN_QKV + 3 * DN_W
OFF_DN_A = OFF_DN_GATE + DN_W
OFF_DN_B = OFF_DN_A + DN_HEADS
OFF_MERGE = OFF_DN_B + DN_HEADS
N_IN = OFF_MERGE + 2 * D_MODEL

P_ATT = 0
P_DN = P_ATT + 3 * ATT_W
P_GATE = P_DN + 3 * DN_W
P_MERGE = P_GATE + DN_W
P_AB = P_MERGE + 2 * D_MODEL
P_TOTAL = P_AB + 128

LANE = 128
SUBLANE = 8
NEG_BIG = -1e30
VMEM_LIMIT = 56 * 1024 * 1024


def _dot(a, b):
    return jnp.dot(a, b, preferred_element_type=F32)


def _dot_nt(a, b):
    return lax.dot_general(a, b, (((1,), (1,)), ((), ())), preferred_element_type=F32)


def _dot_exact(a, b):
    return jnp.dot(a, b, preferred_element_type=F32, precision=lax.Precision.HIGHEST)


def _sigmoid(x):
    return 1.0 / (1.0 + jnp.exp(-x))


def _silu(x):
    return x * _sigmoid(x)


def _norm_mod(x, gain, shift, scale):
    ms = jnp.mean(x * x, axis=-1, keepdims=True)
    y = (x * lax.rsqrt(ms + EPS)) * gain
    return y * (1.0 + scale) + shift


def _const_spec(shape, single_buffer=True):
    nd = len(shape)
    kwargs = {"pipeline_mode": pl.Buffered(1)} if single_buffer else {}
    return pl.BlockSpec(shape, lambda *_: (0,) * nd, **kwargs)


def _mod_spec(k):
    return pl.BlockSpec((1, 1, D_MODEL), lambda b, i: (b, 0, k))


def _ada_kernel(c_ref, w_ref, b_ref, o_ref):
    c = c_ref[...]
    ca = _silu(c).astype(BF16)
    o_ref[0] = _dot(ca, w_ref[0].astype(BF16)) + b_ref[0]


def _ada_mod(c, ada_w, ada_b, tn=1024):
    depth, d, n = ada_w.shape
    bsz = c.shape[0]
    return pl.pallas_call(
        _ada_kernel,
        grid=(depth, n // tn),
        in_specs=[
            pl.BlockSpec((bsz, d), lambda l, j: (0, 0)),
            pl.BlockSpec((1, d, tn), lambda l, j: (l, 0, j)),
            pl.BlockSpec((1, 1, tn), lambda l, j: (l, 0, j)),
        ],
        out_specs=pl.BlockSpec((1, bsz, tn), lambda l, j: (l, 0, j)),
        out_shape=jax.ShapeDtypeStruct((depth, bsz, n), F32),
        name="ada_mod",
    )(c, ada_w, ada_b.reshape(depth, 1, n))


def _ffn_kernel(x_ref, g_ref, sh_ref, sc_ref, gt_ref, wup_ref, wdn_ref, o_ref, *, tf):
    x = x_ref[0]
    hb = _norm_mod(x, g_ref[...], sh_ref[0], sc_ref[0]).astype(BF16)
    acc = jnp.zeros(x.shape, F32)
    for f in range(D_FF // tf):
        gate = _dot(hb, wup_ref[:, f * tf:(f + 1) * tf])
        up = _dot(hb, wup_ref[:, D_FF + f * tf:D_FF + (f + 1) * tf])
        a = (_silu(gate) * up).astype(BF16)
        acc = acc + _dot(a, wdn_ref[f * tf:(f + 1) * tf, :])
    o_ref[0] = x + (0.5 * gt_ref[0]) * acc


def _ffn(x, mod_l, k0, gain, wup, wdn, tm=512, tf=256):
    bsz, t, d = x.shape
    return pl.pallas_call(
        functools.partial(_ffn_kernel, tf=tf),
        grid=(bsz, t // tm),
        in_specs=[
            pl.BlockSpec((1, tm, d), lambda b, i: (b, i, 0)),
            _const_spec((1, d), single_buffer=False),
            _mod_spec(k0), _mod_spec(k0 + 1), _mod_spec(k0 + 2),
            _const_spec((d, 2 * D_FF)),
            _const_spec((D_FF, d)),
        ],
        out_specs=pl.BlockSpec((1, tm, d), lambda b, i: (b, i, 0)),
        out_shape=jax.ShapeDtypeStruct(x.shape, F32),
        compiler_params=pltpu.CompilerParams(
            dimension_semantics=("arbitrary", "arbitrary"), vmem_limit_bytes=VMEM_LIMIT),
        name="ffn",
    )(x, gain.reshape(1, d), mod_l, mod_l, mod_l, wup, wdn)


def _pack_w_in(w):
    q, k, v = w[:, :ATT_W], w[:, ATT_W:2 * ATT_W], w[:, 2 * ATT_W:3 * ATT_W]
    parts = []
    for g in range(N_GROUPS):
        sl = slice(g * ATT_GROUP_W, (g + 1) * ATT_GROUP_W)
        parts += [q[:, sl], k[:, sl], v[:, sl]]
    parts.append(w[:, OFF_DN_QKV:OFF_DN_A])
    parts.append(w[:, OFF_MERGE:])
    parts.append(w[:, OFF_DN_A:OFF_MERGE])
    parts.append(jnp.zeros((w.shape[0], 128 - 2 * DN_HEADS), w.dtype))
    return jnp.concatenate(parts, axis=1).astype(BF16)


def _inproj_kernel(x_ref, g_ref, sh_ref, sc_ref, w_ref, ones_ref, qn_ref, kn_ref, cw_ref,
                   za0_ref, za1_ref, za2_ref, dq_ref, dk_ref, dv_ref, gate_ref, mg_ref,
                   ab_ref, abt_ref, carry_ref, cscr_ref, dscr_ref, *, tm, cn):
    x = x_ref[0]
    hb = _norm_mod(x, g_ref[...], sh_ref[0], sc_ref[0]).astype(BF16)

    @pl.when(pl.program_id(1) == 0)
    def _():
        carry_ref[...] = jnp.zeros(carry_ref.shape, F32)

    jobs = []
    gw = ATT_GROUP_W

    def att_store(za_ref, g, part, val):
        dil = DILATIONS[g]
        if dil == 1:
            za_ref[0, :, part * gw:(part + 1) * gw] = val.astype(BF16)
            return
        for s in range(gw // LANE):
            dscr_ref[s] = val[:, s * LANE:(s + 1) * LANE]
        for r in range(dil):
            for s in range(gw // LANE):
                lo = r * 3 * gw + part * gw + s * LANE
                za_ref[0, :, lo:lo + LANE] = dscr_ref[s, pl.ds(r, tm // dil, stride=dil), :].astype(BF16)

    def att_norm(za_ref, g, part, gain_ref, mult):
        def epilogue(z):
            ms = _dot((z * z).astype(BF16), ones_ref[...])
            att_store(za_ref, g, part, (z * lax.rsqrt(ms + EPS)) * (gain_ref[...] * mult))
        return epilogue

    def att_plain(za_ref, g):
        def epilogue(z):
            att_store(za_ref, g, 2, z)
        return epilogue

    light = []
    for g, za_ref in enumerate((za0_ref, za1_ref, za2_ref)):
        c0 = P_ATT + g * 3 * gw
        light.append((c0, gw, att_norm(za_ref, g, 0, qn_ref, ATT_HEAD_DIM ** -0.5)))
        light.append((c0 + gw, gw, att_norm(za_ref, g, 1, kn_ref, 1.0)))
        light.append((c0 + 2 * gw, gw, att_plain(za_ref, g)))

    def dn_conv(out_ref, part, j):
        dc = part * DN_W + j * cn

        def epilogue(z):
            cscr_ref[0:SUBLANE, :] = carry_ref[:, dc:dc + cn]
            cscr_ref[SUBLANE:SUBLANE + tm, :] = z
            carry_ref[:, dc:dc + cn] = z[tm - SUBLANE:tm, :]
            y = jnp.zeros((tm, cn), F32)
            for i in range(CONV_W):
                s0 = SUBLANE - (CONV_W - 1) + i
                y = y + cw_ref[i:i + 1, dc:dc + cn] * cscr_ref[s0:s0 + tm, :]
            y = _silu(y)
            if part < 2:
                mult = DN_HEAD_DIM ** -0.5 if part == 0 else 1.0
                for h in range(cn // DN_HEAD_DIM):
                    yh = y[:, h * DN_HEAD_DIM:(h + 1) * DN_HEAD_DIM]
                    ss = jnp.sum(yh * yh, axis=-1, keepdims=True)
                    yn = yh * (lax.rsqrt(ss + EPS) * mult)
                    lo = j * cn + h * DN_HEAD_DIM
                    out_ref[0, :, lo:lo + DN_HEAD_DIM] = yn.astype(BF16)
            else:
                out_ref[0, :, j * cn:(j + 1) * cn] = y.astype(BF16)
        return epilogue

    heavy = []
    for part, out_ref in enumerate((dq_ref, dk_ref, dv_ref)):
        for j in range(DN_W // cn):
            heavy.append((P_DN + part * DN_W + j * cn, cn, dn_conv(out_ref, part, j)))

    def store_cast(out_ref, j, fn):
        def epilogue(z):
            out_ref[0, :, j * cn:(j + 1) * cn] = fn(z).astype(BF16)
        return epilogue

    for j in range(DN_W // cn):
        light.append((P_GATE + j * cn, cn, store_cast(gate_ref, j, lambda z: z)))
    for j in range(2 * D_MODEL // cn):
        light.append((P_MERGE + j * cn, cn, store_cast(mg_ref, j, _sigmoid)))

    def ab_epilogue(z):
        ab_ref[0] = z[:, :2 * DN_HEADS]
        abt_ref[0] = z.T[:2 * DN_HEADS, :]

    light.append((P_AB, 128, ab_epilogue))

    per_heavy = -(-len(light) // len(heavy))
    for i, job in enumerate(heavy):
        jobs.append(job)
        jobs.extend(light[i * per_heavy:(i + 1) * per_heavy])

    def project(job):
        return _dot(hb, w_ref[:, job[0]:job[0] + job[1]])

    z = project(jobs[0])
    for i, job in enumerate(jobs):
        z_next = project(jobs[i + 1]) if i + 1 < len(jobs) else None
        job[2](z)
        z = z_next


def _inproj(x, mod_l, gain, w_packed, ones_blk, qn4, kn4, conv_w, tm=512, cn=512):
    bsz, t, d = x.shape
    bf = lambda n: jax.ShapeDtypeStruct((bsz, t, n), BF16)
    row = lambda n: pl.BlockSpec((1, tm, n), lambda b, i: (b, i, 0))
    att_spec = lambda dil: pl.BlockSpec((1, tm // dil, dil * 3 * ATT_GROUP_W), lambda b, i: (b, i, 0))
    att_shape = lambda dil: jax.ShapeDtypeStruct((bsz, t // dil, dil * 3 * ATT_GROUP_W), BF16)
    return pl.pallas_call(
        functools.partial(_inproj_kernel, tm=tm, cn=cn),
        grid=(bsz, t // tm),
        in_specs=[
            row(d),
            _const_spec((1, d), single_buffer=False),
            _mod_spec(3), _mod_spec(4),
            _const_spec((d, P_TOTAL)),
            _const_spec((ATT_GROUP_W, ATT_GROUP_W)),
            _const_spec((1, ATT_GROUP_W), single_buffer=False),
            _const_spec((1, ATT_GROUP_W), single_buffer=False),
            _const_spec((CONV_W, 3 * DN_W), single_buffer=False),
        ],
        out_specs=[att_spec(dil) for dil in DILATIONS] + [row(DN_W)] * 4 + [row(2 * D_MODEL)]
        + [row(2 * DN_HEADS), pl.BlockSpec((1, 2 * DN_HEADS, tm), lambda b, i: (b, 0, i))],
        out_shape=[att_shape(dil) for dil in DILATIONS] + [bf(DN_W)] * 4 + [bf(2 * D_MODEL)]
        + [jax.ShapeDtypeStruct((bsz, t, 2 * DN_HEADS), F32),
           jax.ShapeDtypeStruct((bsz, 2 * DN_HEADS, t), F32)],
        scratch_shapes=[pltpu.VMEM((SUBLANE, 3 * DN_W), F32),
                        pltpu.VMEM((SUBLANE + tm, cn), F32),
                        pltpu.VMEM((ATT_GROUP_W // LANE, tm, LANE), F32)],
        compiler_params=pltpu.CompilerParams(
            dimension_semantics=("arbitrary", "arbitrary"), vmem_limit_bytes=VMEM_LIMIT),
        name="inproj",
    )(x, gain.reshape(1, d), mod_l, mod_l, w_packed, ones_blk, qn4, kn4, conv_w)


def _attn_scores(q, kk, bias, head_masks):
    zero = jnp.zeros_like(q)
    qs = jnp.concatenate([jnp.where(hm, q, zero) for hm in head_masks], axis=0)
    return _dot_nt(qs, kk) + jnp.concatenate([bias] * ATT_HEADS, axis=0)


def _attn_probs(s):
    m = jnp.max(s, axis=-1, keepdims=True)
    p = jnp.exp(s - m)
    l = jnp.sum(p, axis=-1, keepdims=True)
    return p.astype(BF16), 1.0 / l, m + jnp.log(l)


def _attn_out(pv, inv, lse, head_masks):
    nq = pv.shape[0] // ATT_HEADS
    o = jnp.zeros((nq, ATT_GROUP_W), F32)
    ls = jnp.zeros((nq, ATT_GROUP_W), F32)
    for h, hm in enumerate(head_masks):
        rows = slice(h * nq, (h + 1) * nq)
        o = jnp.where(hm, pv[rows] * inv[rows], o)
        ls = jnp.where(hm, jnp.broadcast_to(lse[rows], (nq, ATT_GROUP_W)), ls)
    return o, ls


def _attn_kernel(q1_ref, k1_ref, v1_ref, k1p_ref, v1p_ref, q2_ref, k2_ref, v2_ref, z3_ref,
                 o1_ref, l1_ref, o2_ref, l2_ref, o3_ref, l3_ref):
    nb = ATT_BLOCK
    gw = ATT_GROUP_W
    step = pl.program_id(1)
    lane = lax.broadcasted_iota(jnp.int32, (1, gw), 1)
    head_masks = [(lane // ATT_HEAD_DIM) == h for h in range(ATT_HEADS)]
    qi = lax.broadcasted_iota(jnp.int32, (nb, 2 * nb), 0)
    kj = lax.broadcasted_iota(jnp.int32, (nb, 2 * nb), 1)
    ok_two = (kj >= qi) & (kj <= qi + nb)
    ok_cur = (kj >= nb) & (kj <= qi + nb)
    bias_two = jnp.where(ok_two, 0.0, NEG_BIG).astype(F32)
    bias_first = jnp.where(ok_cur, 0.0, NEG_BIG).astype(F32)
    bias_one = bias_first[:, nb:]

    def blocks(i):
        return slice(i * nb, (i + 1) * nb), slice((i - 1) * nb, (i + 1) * nb)

    def group0():
        jobs = []
        for i in range(4):
            rows, keys = blocks(i)
            if i == 0:
                kk = jnp.concatenate([k1p_ref[0], k1_ref[0, rows]], axis=0)
                vv = jnp.concatenate([v1p_ref[0], v1_ref[0, rows]], axis=0)
                bias = jnp.where(step == 0, bias_first, bias_two)
            else:
                kk, vv, bias = k1_ref[0, keys], v1_ref[0, keys], bias_two
            jobs.append((q1_ref[0, rows], kk, vv, bias, (o1_ref, l1_ref, rows, slice(None))))
        return jobs

    def group1():
        jobs = []
        for i in range(4):
            rows, keys = blocks(i)
            if i == 0:
                kk, vv, bias = k2_ref[0, rows], v2_ref[0, rows], bias_one
            else:
                kk, vv, bias = k2_ref[0, keys], v2_ref[0, keys], bias_two
            jobs.append((q2_ref[0, rows], kk, vv, bias, (o2_ref, l2_ref, rows, slice(None))))
        return jobs

    def group2():
        jobs = []
        for r in range(4):
            c0 = r * 3 * gw
            jobs.append((z3_ref[0, :, c0:c0 + gw], z3_ref[0, :, c0 + gw:c0 + 2 * gw],
                         z3_ref[0, :, c0 + 2 * gw:c0 + 3 * gw], bias_one,
                         (o3_ref, l3_ref, slice(None), slice(r * gw, (r + 1) * gw))))
        return jobs

    def scores(jobs):
        return [_attn_scores(q, kk, bias, head_masks) for q, kk, _, bias, _ in jobs]

    def finish(jobs, ss):
        probs = [_attn_probs(s) for s in ss]
        pvs = [_dot(p, job[2]) for (p, _, _), job in zip(probs, jobs)]
        for pv, (_, inv, lse), job in zip(pvs, probs, jobs):
            o_ref, l_ref, rows, cols = job[4]
            o, ls = _attn_out(pv, inv, lse, head_masks)
            o_ref[0, rows, cols] = o
            l_ref[0, rows, cols] = ls

    j0, j1, j2 = group0(), group1(), group2()
    s0 = scores(j0)
    s1 = scores(j1)
    finish(j0, s0)
    s2 = scores(j2)
    finish(j1, s1)
    finish(j2, s2)


def _attention(za0, z2, z3):
    bsz, t, _ = za0.shape
    gw, nb = ATT_GROUP_W, ATT_BLOCK
    steps = 4
    rows1 = t // steps
    assert rows1 == 4 * nb and t // DILATIONS[1] == 4 * nb and t // DILATIONS[2] == nb
    prev = lambda b, j: (b, jnp.maximum(4 * j - 1, 0), 1)
    prev_v = lambda b, j: (b, jnp.maximum(4 * j - 1, 0), 2)
    in_specs = [
        pl.BlockSpec((1, rows1, gw), lambda b, j: (b, j, 0)),
        pl.BlockSpec((1, rows1, gw), lambda b, j: (b, j, 1)),
        pl.BlockSpec((1, rows1, gw), lambda b, j: (b, j, 2)),
        pl.BlockSpec((1, nb, gw), prev),
        pl.BlockSpec((1, nb, gw), prev_v),
        pl.BlockSpec((1, 4 * nb, gw), lambda b, j: (b, 0, 3 * j)),
        pl.BlockSpec((1, 4 * nb, gw), lambda b, j: (b, 0, 3 * j + 1)),
        pl.BlockSpec((1, 4 * nb, gw), lambda b, j: (b, 0, 3 * j + 2)),
        pl.BlockSpec((1, nb, 4 * 3 * gw), lambda b, j: (b, 0, j)),
    ]
    o1s = pl.BlockSpec((1, rows1, gw), lambda b, j: (b, j, 0))
    o2s = pl.BlockSpec((1, 4 * nb, gw), lambda b, j: (b, 0, j))
    o3s = pl.BlockSpec((1, nb, 4 * gw), lambda b, j: (b, 0, j))
    s1 = jax.ShapeDtypeStruct((bsz, t, gw), F32)
    s2 = jax.ShapeDtypeStruct((bsz, t // DILATIONS[1], DILATIONS[1] * gw), F32)
    s3 = jax.ShapeDtypeStruct((bsz, t // DILATIONS[2], DILATIONS[2] * gw), F32)
    o1, l1, o2, l2, o3, l3 = pl.pallas_call(
        _attn_kernel,
        grid=(bsz, steps),
        in_specs=in_specs,
        out_specs=[o1s, o1s, o2s, o2s, o3s, o3s],
        out_shape=[s1, s1, s2, s2, s3, s3],
        compiler_params=pltpu.CompilerParams(
            dimension_semantics=("arbitrary", "arbitrary"), vmem_limit_bytes=VMEM_LIMIT),
        name="dilated_attn",
    )(za0, za0, za0, za0, za0, z2, z2, z2, z3)
    return o1, l1, o2, l2, o3, l3


def _softplus(x):
    return jnp.maximum(x, 0.0) + jnp.log(1.0 + jnp.exp(-jnp.abs(x)))


def _chunk_masks():
    c = DN_CHUNK
    ii = lax.broadcasted_iota(jnp.int32, (c, c), 0)
    jj = lax.broadcasted_iota(jnp.int32, (c, c), 1)
    return ii >= jj, ii > jj, ii <= jj


def _chunk_decay(ab, abt, alog_ref, dtb_ref, alog_c_ref, dtb_c_ref, tril, triu):
    g_col = -jnp.exp(alog_ref[...]) * _softplus(ab + dtb_ref[...])
    g_row = -jnp.exp(alog_c_ref[...]) * _softplus(abt + dtb_c_ref[...])
    gc_col = _dot_exact(jnp.where(tril, 1.0, 0.0).astype(F32), g_col)
    gc_row = _dot_exact(g_row, jnp.where(triu, 1.0, 0.0).astype(F32))
    return gc_col, gc_row, _sigmoid(ab)


def _head_decay(gc_col, gc_row, beta_col, h, tril):
    c, hd = DN_CHUNK, DN_HEAD_DIM
    cb = jnp.broadcast_to(gc_col[:, h:h + 1], (c, hd))
    bb = jnp.broadcast_to(beta_col[:, DN_HEADS + h:DN_HEADS + h + 1], (c, hd))
    rb = jnp.broadcast_to(gc_row[h:h + 1, :], (c, c))
    ld = jnp.exp(jnp.where(tril, cb[:, :c] - rb, 0.0))
    return cb, bb, ld


def _dn_prep_kernel(k_ref, ab_ref, abt_ref, alog_ref, dtb_ref, alog_c_ref, dtb_c_ref, a_ref,
                    *, n_chunks):
    c, hd = DN_CHUNK, DN_HEAD_DIM
    tril, strict, triu = _chunk_masks()
    for n in range(n_chunks):
        r0 = n * c
        gc_col, gc_row, beta_col = _chunk_decay(ab_ref[0, r0:r0 + c, :], abt_ref[0, n], alog_ref,
                                                dtb_ref, alog_c_ref, dtb_c_ref, tril, triu)
        for h in range(DN_HEADS):
            k = k_ref[0, r0:r0 + c, h * hd:(h + 1) * hd]
            _, bb, ld = _head_decay(gc_col, gc_row, beta_col, h, tril)
            a_ref[0, n, h] = jnp.where(strict, _dot_nt(k, k) * (bb[:, :c] * ld), 0.0)


def _dn_solve_kernel(a_ref, t_ref, at_scr, tt_scr):
    c = DN_CHUNK
    tiles = c * c // LANE
    per = LANE // c
    for t in range(tiles):
        blk = a_ref[:, t * LANE:(t + 1) * LANE].T
        for r in range(per):
            at_scr[t * per + r] = blk[r * c:(r + 1) * c]
    row_id = lax.broadcasted_iota(jnp.int32, (SUBLANE, LANE), 0)
    zero = jnp.zeros((SUBLANE, LANE), F32)
    for i in range(c):
        groups = i // SUBLANE + 1
        acc = [zero] * (c // SUBLANE)
        acc[groups - 1] = jnp.where(row_id == i % SUBLANE, 1.0, 0.0).astype(F32)
        for j in range(i):
            a = jnp.broadcast_to(at_scr[i, j:j + 1, :], (SUBLANE, LANE))
            for g in range(j // SUBLANE + 1):
                acc[g] = acc[g] - a * tt_scr[j, g * SUBLANE:(g + 1) * SUBLANE, :]
        for g in range(c // SUBLANE):
            tt_scr[i, g * SUBLANE:(g + 1) * SUBLANE, :] = acc[g]
    for t in range(tiles):
        blk = jnp.concatenate([tt_scr[t * per + r] for r in range(per)], axis=0)
        t_ref[:, t * LANE:(t + 1) * LANE] = blk.T


def _dn_scan_kernel(q_ref, k_ref, v_ref, gate_ref, ab_ref, abt_ref, t_ref, alog_ref, dtb_ref,
                    alog_c_ref, dtb_c_ref, nrm_ref, o_ref, s_ref, *, n_chunks):
    c, hd = DN_CHUNK, DN_HEAD_DIM

    @pl.when(pl.program_id(1) == 0)
    def _():
        s_ref[...] = jnp.zeros(s_ref.shape, F32)

    tril, _, triu = _chunk_masks()
    heads = range(DN_HEADS)
    pre = []
    for n in range(n_chunks):
        r0 = n * c
        gc_col, gc_row, beta_col = _chunk_decay(ab_ref[0, r0:r0 + c, :], abt_ref[0, n], alog_ref,
                                                dtb_ref, alog_c_ref, dtb_c_ref, tril, triu)
        dec, qk, xs = [], [], []
        for h in heads:
            dec.append(_head_decay(gc_col, gc_row, beta_col, h, tril))
        for h in heads:
            lanes = slice(h * hd, (h + 1) * hd)
            qk.append(_dot_nt(q_ref[0, r0:r0 + c, lanes], k_ref[0, r0:r0 + c, lanes]))
        for h in heads:
            lanes = slice(h * hd, (h + 1) * hd)
            cb, bb, _ = dec[h]
            k = k_ref[0, r0:r0 + c, lanes].astype(F32)
            v = v_ref[0, r0:r0 + c, lanes].astype(F32)
            rhs = jnp.concatenate([v * bb, (k * bb) * jnp.exp(cb)], axis=1).astype(BF16)
            xs.append(_dot(t_ref[0, n, h].astype(BF16), rhs))
        per_head = []
        for h in heads:
            lanes = slice(h * hd, (h + 1) * hd)
            cb, _, ld = dec[h]
            q = q_ref[0, r0:r0 + c, lanes].astype(F32)
            k = k_ref[0, r0:r0 + c, lanes].astype(F32)
            gl = cb[c - 1:c, :]
            attn = jnp.where(tril, qk[h] * ld, 0.0).astype(BF16)
            kdec_t = (k * jnp.exp(gl - cb)).T.astype(BF16)
            lhs_s = jnp.concatenate([xs[h][:, hd:], q * jnp.exp(cb)], axis=0).astype(BF16)
            per_head.append((xs[h][:, :hd], lhs_s, jnp.concatenate([attn, kdec_t], axis=0),
                             jnp.exp(gl)))
        pre.append(per_head)

    state = [s_ref[h] for h in heads]
    for n in range(n_chunks):
        r0 = n * c
        ws = [_dot(pre[n][h][1], state[h].astype(BF16)) for h in heads]
        vn = [(pre[n][h][0] - ws[h][:c]).astype(BF16) for h in heads]
        upd = [_dot(pre[n][h][2], vn[h]) for h in heads]
        for h in heads:
            lanes = slice(h * hd, (h + 1) * hd)
            o = ws[h][c:] + upd[h][:c]
            state[h] = state[h] * pre[n][h][3] + upd[h][c:]
            ms = jnp.mean(o * o, axis=-1, keepdims=True)
            on = (o * lax.rsqrt(ms + EPS)) * nrm_ref[...]
            gate = gate_ref[0, r0:r0 + c, lanes].astype(F32)
            o_ref[0, r0:r0 + c, lanes] = (on * _silu(gate)).astype(BF16)
    for h in heads:
        s_ref[h] = state[h]


def _deltanet(dq, dk, dv, gate, ab, abt, a_log, dt_bias, dn_norm, rows=256):
    bsz, t, w = dq.shape
    c = DN_CHUNK
    n_chunks = rows // c
    total_chunks = t // c
    abt_c = abt.reshape(bsz, 2 * DN_HEADS, total_chunks, c).transpose(0, 2, 1, 3)
    pad = jnp.zeros((DN_HEADS,), F32)
    alog16 = jnp.concatenate([a_log, pad])
    dtb16 = jnp.concatenate([dt_bias, pad])
    small = [alog16.reshape(1, -1), dtb16.reshape(1, -1), alog16.reshape(-1, 1), dtb16.reshape(-1, 1)]
    small_specs = [_const_spec((1, 2 * DN_HEADS), single_buffer=False)] * 2 + \
                  [_const_spec((2 * DN_HEADS, 1), single_buffer=False)] * 2
    row = pl.BlockSpec((1, rows, w), lambda b, i: (b, i, 0))
    ab_spec = pl.BlockSpec((1, rows, 2 * DN_HEADS), lambda b, i: (b, i, 0))
    abt_spec = pl.BlockSpec((1, n_chunks, 2 * DN_HEADS, c), lambda b, i: (b, i, 0, 0))
    mat_spec = pl.BlockSpec((1, n_chunks, DN_HEADS, c, c), lambda b, i: (b, i, 0, 0, 0))
    mat_shape = jax.ShapeDtypeStruct((bsz, total_chunks, DN_HEADS, c, c), F32)
    params = pltpu.CompilerParams(dimension_semantics=("arbitrary", "arbitrary"),
                                  vmem_limit_bytes=VMEM_LIMIT)

    a_mat = pl.pallas_call(
        functools.partial(_dn_prep_kernel, n_chunks=n_chunks),
        grid=(bsz, t // rows),
        in_specs=[row, ab_spec, abt_spec] + small_specs,
        out_specs=mat_spec,
        out_shape=mat_shape,
        compiler_params=params,
        name="dn_prep",
    )(dk, ab, abt_c, *small)

    n_sys = bsz * total_chunks * DN_HEADS
    assert n_sys % LANE == 0
    flat_spec = pl.BlockSpec((LANE, c * c), lambda i: (i, 0))
    t_mat = pl.pallas_call(
        _dn_solve_kernel,
        grid=(n_sys // LANE,),
        in_specs=[flat_spec],
        out_specs=flat_spec,
        out_shape=jax.ShapeDtypeStruct((n_sys, c * c), F32),
        scratch_shapes=[pltpu.VMEM((c, c, LANE), F32), pltpu.VMEM((c, c, LANE), F32)],
        compiler_params=pltpu.CompilerParams(dimension_semantics=("arbitrary",),
                                             vmem_limit_bytes=VMEM_LIMIT),
        name="dn_solve",
    )(a_mat.reshape(n_sys, c * c)).reshape(mat_shape.shape)

    return pl.pallas_call(
        functools.partial(_dn_scan_kernel, n_chunks=n_chunks),
        grid=(bsz, t // rows),
        in_specs=[row, row, row, row, ab_spec, abt_spec, mat_spec] + small_specs
        + [_const_spec((1, DN_HEAD_DIM), single_buffer=False)],
        out_specs=row,
        out_shape=jax.ShapeDtypeStruct((bsz, t, w), BF16),
        scratch_shapes=[pltpu.VMEM((DN_HEADS, DN_HEAD_DIM, DN_HEAD_DIM), F32)],
        compiler_params=params,
        name="dn_scan",
    )(dq, dk, dv, gate, ab, abt_c, t_mat, *small, dn_norm.reshape(1, -1))


def _merge_kernel(o1_ref, l1_ref, o2_ref, l2_ref, o3_ref, l3_ref, odn_ref, mg_ref, x_ref, gt_ref,
                  wpa_ref, wpd_ref, wo_ref, out_ref, nat_scr):
    d = D_MODEL
    tm = x_ref.shape[1]
    slabs = ATT_GROUP_W // LANE
    for a, (ref, dil) in enumerate(((o2_ref, DILATIONS[1]), (l2_ref, DILATIONS[1]),
                                    (o3_ref, DILATIONS[2]), (l3_ref, DILATIONS[2]))):
        for r in range(dil):
            for s in range(slabs):
                lo = r * ATT_GROUP_W + s * LANE
                nat_scr[a, s, pl.ds(r, tm // dil, stride=dil), :] = ref[0, :, lo:lo + LANE]
    halves = [slice(i * (tm // 2), (i + 1) * (tm // 2)) for i in range(2)]
    yd = [_dot(odn_ref[0, r], wpd_ref[...]) for r in halves]
    ya = []
    for r in halves:
        ys = []
        for s in range(slabs):
            cols = slice(s * LANE, (s + 1) * LANE)
            l1, l2, l3 = l1_ref[0, r, cols], nat_scr[1, s, r, :], nat_scr[3, s, r, :]
            m = jnp.maximum(jnp.maximum(l1, l2), l3)
            e1, e2, e3 = jnp.exp(l1 - m), jnp.exp(l2 - m), jnp.exp(l3 - m)
            y = e1 * o1_ref[0, r, cols] + e2 * nat_scr[0, s, r, :] + e3 * nat_scr[2, s, r, :]
            ys.append((y / (e1 + e2 + e3)).astype(BF16))
        ya.append(_dot(jnp.concatenate(ys, axis=1), wpa_ref[...]))
    for i, r in enumerate(halves):
        mix = mg_ref[0, r, :d].astype(F32) * ya[i] + mg_ref[0, r, d:].astype(F32) * yd[i]
        out = _dot(mix.astype(BF16), wo_ref[...])
        out_ref[0, r] = x_ref[0, r] + gt_ref[0] * out


def _merge(att, odn, mg, x, mod_l, wpa, wpd, wo, tm=512):
    bsz, t, d = x.shape
    row = lambda n: pl.BlockSpec((1, tm, n), lambda b, i: (b, i, 0))
    view = lambda dil: pl.BlockSpec((1, tm // dil, dil * ATT_GROUP_W), lambda b, i: (b, i, 0))
    att_specs = [row(ATT_GROUP_W)] * 2 + [view(DILATIONS[1])] * 2 + [view(DILATIONS[2])] * 2
    return pl.pallas_call(
        _merge_kernel,
        grid=(bsz, t // tm),
        in_specs=att_specs + [row(DN_W), row(2 * d), row(d), _mod_spec(5),
                              _const_spec(wpa.shape), _const_spec(wpd.shape),
                              _const_spec(wo.shape)],
        out_specs=row(d),
        out_shape=jax.ShapeDtypeStruct(x.shape, F32),
        scratch_shapes=[pltpu.VMEM((4, ATT_GROUP_W // LANE, tm, LANE), F32)],
        compiler_params=pltpu.CompilerParams(
            dimension_semantics=("arbitrary", "arbitrary"), vmem_limit_bytes=VMEM_LIMIT),
        name="merge_out",
    )(*att, odn, mg, x, mod_l, wpa, wpd, wo)


def _mixer(x, mod_l, gain, w_packed, ones_blk, qn4, kn4, conv_w, a_log, dt_bias, dn_norm,
           wpa, wpd, wo):
    za0, za1, za2, dq, dk, dv, gate, mg, ab, abt = _inproj(
        x, mod_l, gain, w_packed, ones_blk, qn4, kn4, conv_w)
    att = _attention(za0, za1, za2)
    odn = _deltanet(dq, dk, dv, gate, ab, abt, a_log, dt_bias, dn_norm)
    return _merge(att, odn, mg, x, mod_l, wpa, wpd, wo)


def kernel(x, c, ada_w, ada_b, norm_ff1, ffn1_w_up, ffn1_w_down, norm_mix, w_in, q_norm, k_norm,
           conv_w, a_log, dt_bias, dn_norm, w_proj_att, w_proj_dn, w_out, norm_ff2, ffn2_w_up,
           ffn2_w_down):
    bsz = x.shape[0]
    mod = _ada_mod(c, ada_w, ada_b)
    blk = jnp.arange(ATT_GROUP_W) // ATT_HEAD_DIM
    ones_blk = jnp.where(blk[:, None] == blk[None, :], 1.0 / ATT_HEAD_DIM, 0.0).astype(BF16)
    for l in range(DEPTH):
        mod_l = mod[l].reshape(bsz, 1, N_ADA * D_MODEL)
        x = _ffn(x, mod_l, 0, norm_ff1[l], ffn1_w_up[l].astype(BF16), ffn1_w_down[l].astype(BF16))
        x = _mixer(x, mod_l, norm_mix[l], _pack_w_in(w_in[l]), ones_blk,
                   jnp.tile(q_norm[l], ATT_HEADS).reshape(1, -1),
                   jnp.tile(k_norm[l], ATT_HEADS).reshape(1, -1),
                   conv_w[l], a_log[l], dt_bias[l], dn_norm[l],
                   w_proj_att[l].astype(BF16), w_proj_dn[l].astype(BF16), w_out[l].astype(BF16))
        x = _ffn(x, mod_l, 6, norm_ff2[l], ffn2_w_up[l].astype(BF16), ffn2_w_down[l].astype(BF16))
    return x
```

```python
import functools

import jax
import jax.numpy as jnp
from jax import lax
from jax.experimental import pallas as pl
from jax.experimental.pallas import tpu as pltpu

F32 = jnp.float32
BF16 = jnp.bfloat16

D_MODEL = 1024
DEPTH = 4
SEQ = 2048
D_FF = 2816
EPS = 1e-6
N_ADA = 9

ATT_HEAD_DIM = 64
ATT_HEADS = 4
ATT_GROUP_W = ATT_HEADS * ATT_HEAD_DIM
DILATIONS = (1, 4, 16)
ATT_BLOCK = 128
N_GROUPS = 3
ATT_W = N_GROUPS * ATT_GROUP_W

DN_HEADS = 8
DN_HEAD_DIM = 128
DN_W = DN_HEADS * DN_HEAD_DIM
DN_CHUNK = 64
CONV_W = 4

OFF_DN_QKV = 3 * ATT_W
OFF_DN_GATE = OFF_DN_QKV + 3 * DN_W
OFF_DN_A = OFF_DN_GATE + DN_W
OFF_DN_B = OFF_DN_A + DN_HEADS
OFF_MERGE = OFF_DN_B + DN_HEADS
N_IN = OFF_MERGE + 2 * D_MODEL

P_ATT = 0
P_DN = P_ATT + 3 * ATT_W
P_GATE = P_DN + 3 * DN_W
P_MERGE = P_GATE + DN_W
P_AB = P_MERGE + 2 * D_MODEL
P_TOTAL = P_AB + 128

LANE = 128
SUBLANE = 8
NEG_BIG = -1e30
VMEM_LIMIT = 56 * 1024 * 1024
INPROJ_LOOKAHEAD = 1


def _dot(a, b):
    return jnp.dot(a, b, preferred_element_type=F32)


def _dot_nt(a, b):
    return lax.dot_general(a, b, (((1,), (1,)), ((), ())), preferred_element_type=F32)


def _dot_exact(a, b):
    return jnp.dot(a, b, preferred_element_type=F32, precision=lax.Precision.HIGHEST)


def _sigmoid(x):
    return 1.0 / (1.0 + jnp.exp(-x))


def _silu(x):
    return x * _sigmoid(x)


def _norm_mod(x, gain, shift, scale):
    ms = jnp.mean(x * x, axis=-1, keepdims=True)
    y = (x * lax.rsqrt(ms + EPS)) * gain
    return y * (1.0 + scale) + shift


def _const_spec(shape, single_buffer=True):
    nd = len(shape)
    kwargs = {"pipeline_mode": pl.Buffered(1)} if single_buffer else {}
    return pl.BlockSpec(shape, lambda *_: (0,) * nd, **kwargs)


def _mod_spec(k):
    return pl.BlockSpec((1, 1, D_MODEL), lambda b, i: (b, 0, k))


def _ada_kernel(c_ref, w_ref, b_ref, o_ref):
    c = c_ref[...]
    ca = _silu(c).astype(BF16)
    o_ref[0] = _dot(ca, w_ref[0].astype(BF16)) + b_ref[0]


def _ada_mod(c, ada_w, ada_b, tn=1024):
    depth, d, n = ada_w.shape
    bsz = c.shape[0]
    return pl.pallas_call(
        _ada_kernel,
        grid=(depth, n // tn),
        in_specs=[
            pl.BlockSpec((bsz, d), lambda l, j: (0, 0)),
            pl.BlockSpec((1, d, tn), lambda l, j: (l, 0, j)),
            pl.BlockSpec((1, 1, tn), lambda l, j: (l, 0, j)),
        ],
        out_specs=pl.BlockSpec((1, bsz, tn), lambda l, j: (l, 0, j)),
        out_shape=jax.ShapeDtypeStruct((depth, bsz, n), F32),
        name="ada_mod",
    )(c, ada_w, ada_b.reshape(depth, 1, n))


def _ffn_kernel(x_ref, g_ref, sh_ref, sc_ref, gt_ref, wup_ref, wdn_ref, o_ref, *, tf):
    x = x_ref[0]
    hb = _norm_mod(x, g_ref[...], sh_ref[0], sc_ref[0]).astype(BF16)
    acc = jnp.zeros(x.shape, F32)
    for f in range(D_FF // tf):
        gate = _dot(hb, wup_ref[:, f * tf:(f + 1) * tf])
        up = _dot(hb, wup_ref[:, D_FF + f * tf:D_FF + (f + 1) * tf])
        a = (_silu(gate) * up).astype(BF16)
        acc = acc + _dot(a, wdn_ref[f * tf:(f + 1) * tf, :])
    o_ref[0] = x + (0.5 * gt_ref[0]) * acc


def _ffn(x, mod_l, k0, gain, wup, wdn, tm=512, tf=256):
    bsz, t, d = x.shape
    return pl.pallas_call(
        functools.partial(_ffn_kernel, tf=tf),
        grid=(bsz, t // tm),
        in_specs=[
            pl.BlockSpec((1, tm, d), lambda b, i: (b, i, 0)),
            _const_spec((1, d), single_buffer=False),
            _mod_spec(k0), _mod_spec(k0 + 1), _mod_spec(k0 + 2),
            _const_spec((d, 2 * D_FF)),
            _const_spec((D_FF, d)),
        ],
        out_specs=pl.BlockSpec((1, tm, d), lambda b, i: (b, i, 0)),
        out_shape=jax.ShapeDtypeStruct(x.shape, F32),
        compiler_params=pltpu.CompilerParams(
            dimension_semantics=("arbitrary", "arbitrary"), vmem_limit_bytes=VMEM_LIMIT),
        name="ffn",
    )(x, gain.reshape(1, d), mod_l, mod_l, mod_l, wup, wdn)


def _pack_w_in(w):
    q, k, v = w[..., :ATT_W], w[..., ATT_W:2 * ATT_W], w[..., 2 * ATT_W:3 * ATT_W]
    parts = []
    for g in range(N_GROUPS):
        sl = slice(g * ATT_GROUP_W, (g + 1) * ATT_GROUP_W)
        parts += [q[..., sl], k[..., sl], v[..., sl]]
    parts.append(w[..., OFF_DN_QKV:OFF_DN_A])
    parts.append(w[..., OFF_MERGE:])
    parts.append(w[..., OFF_DN_A:OFF_MERGE])
    parts.append(jnp.zeros(w.shape[:-1] + (128 - 2 * DN_HEADS,), w.dtype))
    return jnp.concatenate([p.astype(BF16) for p in parts], axis=-1)


def _inproj_kernel(x_ref, g_ref, sh_ref, sc_ref, w_ref, ones_ref, qn_ref, kn_ref, cw_ref,
                   za0_ref, za1_ref, za2_ref, dq_ref, dk_ref, dv_ref, gate_ref, mg_ref,
                   ab_ref, abt_ref, carry_ref, dscr_ref, *, tm, cn):
    x = x_ref[0]
    hb = _norm_mod(x, g_ref[...], sh_ref[0], sc_ref[0]).astype(BF16)

    @pl.when(pl.program_id(1) == 0)
    def _():
        carry_ref[...] = jnp.zeros(carry_ref.shape, F32)

    jobs = []
    gw = ATT_GROUP_W

    def att_store(za_ref, g, part, val):
        dil = DILATIONS[g]
        if dil == 1:
            za_ref[0, :, part * gw:(part + 1) * gw] = val.astype(BF16)
            return
        for s in range(gw // LANE):
            dscr_ref[s] = val[:, s * LANE:(s + 1) * LANE]
        for r in range(dil):
            for s in range(gw // LANE):
                lo = r * 3 * gw + part * gw + s * LANE
                za_ref[0, :, lo:lo + LANE] = dscr_ref[s, pl.ds(r, tm // dil, stride=dil), :].astype(BF16)

    def att_norm(za_ref, g, part, gain_ref, mult):
        def epilogue(z):
            ms = _dot((z * z).astype(BF16), ones_ref[...])
            att_store(za_ref, g, part, (z * lax.rsqrt(ms + EPS)) * (gain_ref[...] * mult))
        return epilogue

    def att_plain(za_ref, g):
        def epilogue(z):
            att_store(za_ref, g, 2, z)
        return epilogue

    light = []
    for g, za_ref in enumerate((za0_ref, za1_ref, za2_ref)):
        c0 = P_ATT + g * 3 * gw
        light.append((c0, gw, att_norm(za_ref, g, 0, qn_ref, ATT_HEAD_DIM ** -0.5)))
        light.append((c0 + gw, gw, att_norm(za_ref, g, 1, kn_ref, 1.0)))
        light.append((c0 + 2 * gw, gw, att_plain(za_ref, g)))

    def dn_conv(out_ref, part, j):
        dc = part * DN_W + j * cn

        def epilogue(z):
            xe = jnp.concatenate([carry_ref[:, dc:dc + cn], z], axis=0)
            carry_ref[:, dc:dc + cn] = z[tm - SUBLANE:tm, :]
            y = cw_ref[CONV_W - 1:CONV_W, dc:dc + cn] * z
            for s in range(1, CONV_W):
                shifted = pltpu.roll(xe, s, axis=0)[SUBLANE:, :]
                y = y + cw_ref[CONV_W - 1 - s:CONV_W - s, dc:dc + cn] * shifted
            y = _silu(y)
            if part < 2:
                mult = DN_HEAD_DIM ** -0.5 if part == 0 else 1.0
                for h in range(cn // DN_HEAD_DIM):
                    yh = y[:, h * DN_HEAD_DIM:(h + 1) * DN_HEAD_DIM]
                    ss = jnp.sum(yh * yh, axis=-1, keepdims=True)
                    yn = yh * (lax.rsqrt(ss + EPS) * mult)
                    lo = j * cn + h * DN_HEAD_DIM
                    out_ref[0, :, lo:lo + DN_HEAD_DIM] = yn.astype(BF16)
            else:
                out_ref[0, :, j * cn:(j + 1) * cn] = y.astype(BF16)
        return epilogue

    heavy = []
    for part, out_ref in enumerate((dq_ref, dk_ref, dv_ref)):
        for j in range(DN_W // cn):
            heavy.append((P_DN + part * DN_W + j * cn, cn, dn_conv(out_ref, part, j)))

    def store_cast(out_ref, j, fn):
        def epilogue(z):
            out_ref[0, :, j * cn:(j + 1) * cn] = fn(z).astype(BF16)
        return epilogue

    for j in range(DN_W // cn):
        light.append((P_GATE + j * cn, cn, store_cast(gate_ref, j, lambda z: z)))
    for j in range(2 * D_MODEL // cn):
        light.append((P_MERGE + j * cn, cn, store_cast(mg_ref, j, _sigmoid)))

    def ab_epilogue(z):
        ab_ref[0] = z[:, :2 * DN_HEADS]
        abt_ref[0] = z.T[:2 * DN_HEADS, :]

    light.append((P_AB, 128, ab_epilogue))

    per_heavy = -(-len(light) // len(heavy))
    for i, job in enumerate(heavy):
        jobs.append(job)
        jobs.extend(light[i * per_heavy:(i + 1) * per_heavy])

    def project(job):
        return _dot(hb, w_ref[:, job[0]:job[0] + job[1]])

    pending = [project(job) for job in jobs[:INPROJ_LOOKAHEAD]]
    for i, job in enumerate(jobs):
        if i + INPROJ_LOOKAHEAD < len(jobs):
            pending.append(project(jobs[i + INPROJ_LOOKAHEAD]))
        job[2](pending.pop(0))


def _inproj(x, mod_l, gain, w_packed, ones_blk, qn4, kn4, conv_w, tm=512, cn=512):
    bsz, t, d = x.shape
    bf = lambda n: jax.ShapeDtypeStruct((bsz, t, n), BF16)
    row = lambda n: pl.BlockSpec((1, tm, n), lambda b, i: (b, i, 0))
    att_spec = lambda dil: pl.BlockSpec((1, tm // dil, dil * 3 * ATT_GROUP_W), lambda b, i: (b, i, 0))
    att_shape = lambda dil: jax.ShapeDtypeStruct((bsz, t // dil, dil * 3 * ATT_GROUP_W), BF16)
    return pl.pallas_call(
        functools.partial(_inproj_kernel, tm=tm, cn=cn),
        grid=(bsz, t // tm),
        in_specs=[
            row(d),
            _const_spec((1, d), single_buffer=False),
            _mod_spec(3), _mod_spec(4),
            _const_spec((d, P_TOTAL)),
            _const_spec((ATT_GROUP_W, ATT_GROUP_W)),
            _const_spec((1, ATT_GROUP_W), single_buffer=False),
            _const_spec((1, ATT_GROUP_W), single_buffer=False),
            _const_spec((CONV_W, 3 * DN_W), single_buffer=False),
        ],
        out_specs=[att_spec(dil) for dil in DILATIONS] + [row(DN_W)] * 4 + [row(2 * D_MODEL)]
        + [row(2 * DN_HEADS), pl.BlockSpec((1, 2 * DN_HEADS, tm), lambda b, i: (b, 0, i))],
        out_shape=[att_shape(dil) for dil in DILATIONS] + [bf(DN_W)] * 4 + [bf(2 * D_MODEL)]
        + [jax.ShapeDtypeStruct((bsz, t, 2 * DN_HEADS), F32),
           jax.ShapeDtypeStruct((bsz, 2 * DN_HEADS, t), F32)],
        scratch_shapes=[pltpu.VMEM((SUBLANE, 3 * DN_W), F32),
                        pltpu.VMEM((ATT_GROUP_W // LANE, tm, LANE), F32)],
        compiler_params=pltpu.CompilerParams(
            dimension_semantics=("arbitrary", "arbitrary"), vmem_limit_bytes=VMEM_LIMIT),
        name="inproj",
    )(x, gain.reshape(1, d), mod_l, mod_l, w_packed, ones_blk, qn4, kn4, conv_w)


def _attn_scores(q, kk, bias, head_masks):
    zero = jnp.zeros_like(q)
    qs = jnp.concatenate([jnp.where(hm, q, zero) for hm in head_masks], axis=0)
    return _dot_nt(qs, kk) + jnp.concatenate([bias] * ATT_HEADS, axis=0)


def _attn_probs(s):
    m = jnp.max(s, axis=-1, keepdims=True)
    p = jnp.exp(s - m)
    l = jnp.sum(p, axis=-1, keepdims=True)
    return p.astype(BF16), 1.0 / l, m + jnp.log(l)


def _attn_out(pv, inv, lse, head_masks):
    nq = pv.shape[0] // ATT_HEADS
    o = jnp.zeros((nq, ATT_GROUP_W), F32)
    ls = jnp.zeros((nq, ATT_GROUP_W), F32)
    for h, hm in enumerate(head_masks):
        rows = slice(h * nq, (h + 1) * nq)
        o = jnp.where(hm, pv[rows] * inv[rows], o)
        ls = jnp.where(hm, jnp.broadcast_to(lse[rows], (nq, ATT_GROUP_W)), ls)
    return o, ls


def _attn_kernel(q1_ref, k1_ref, v1_ref, k1p_ref, v1p_ref, q2_ref, k2_ref, v2_ref, z3_ref,
                 o1_ref, l1_ref, o2_ref, l2_ref, o3_ref, l3_ref):
    nb = ATT_BLOCK
    gw = ATT_GROUP_W
    step = pl.program_id(1)
    lane = lax.broadcasted_iota(jnp.int32, (1, gw), 1)
    head_masks = [(lane // ATT_HEAD_DIM) == h for h in range(ATT_HEADS)]
    qi = lax.broadcasted_iota(jnp.int32, (nb, 2 * nb), 0)
    kj = lax.broadcasted_iota(jnp.int32, (nb, 2 * nb), 1)
    ok_two = (kj >= qi) & (kj <= qi + nb)
    ok_cur = (kj >= nb) & (kj <= qi + nb)
    bias_two = jnp.where(ok_two, 0.0, NEG_BIG).astype(F32)
    bias_first = jnp.where(ok_cur, 0.0, NEG_BIG).astype(F32)
    bias_one = bias_first[:, nb:]

    def blocks(i):
        return slice(i * nb, (i + 1) * nb), slice((i - 1) * nb, (i + 1) * nb)

    def group0():
        jobs = []
        for i in range(4):
            rows, keys = blocks(i)
            if i == 0:
                kk = jnp.concatenate([k1p_ref[0], k1_ref[0, rows]], axis=0)
                vv = jnp.concatenate([v1p_ref[0], v1_ref[0, rows]], axis=0)
                bias = jnp.where(step == 0, bias_first, bias_two)
            else:
                kk, vv, bias = k1_ref[0, keys], v1_ref[0, keys], bias_two
            jobs.append((q1_ref[0, rows], kk, vv, bias, (o1_ref, l1_ref, rows, slice(None))))
        return jobs

    def group1():
        jobs = []
        for i in range(4):
            rows, keys = blocks(i)
            if i == 0:
                kk, vv, bias = k2_ref[0, rows], v2_ref[0, rows], bias_one
            else:
                kk, vv, bias = k2_ref[0, keys], v2_ref[0, keys], bias_two
            jobs.append((q2_ref[0, rows], kk, vv, bias, (o2_ref, l2_ref, rows, slice(None))))
        return jobs

    def group2():
        jobs = []
        for r in range(4):
            c0 = r * 3 * gw
            jobs.append((z3_ref[0, :, c0:c0 + gw], z3_ref[0, :, c0 + gw:c0 + 2 * gw],
                         z3_ref[0, :, c0 + 2 * gw:c0 + 3 * gw], bias_one,
                         (o3_ref, l3_ref, slice(None), slice(r * gw, (r + 1) * gw))))
        return jobs

    def scores(jobs):
        return [_attn_scores(q, kk, bias, head_masks) for q, kk, _, bias, _ in jobs]

    def finish(jobs, ss):
        probs = [_attn_probs(s) for s in ss]
        pvs = [_dot(p, job[2]) for (p, _, _), job in zip(probs, jobs)]
        for pv, (_, inv, lse), job in zip(pvs, probs, jobs):
            o_ref, l_ref, rows, cols = job[4]
            o, ls = _attn_out(pv, inv, lse, head_masks)
            o_ref[0, rows, cols] = o
            l_ref[0, rows, cols] = ls

    j0, j1, j2 = group0(), group1(), group2()
    s0 = scores(j0)
    s1 = scores(j1)
    finish(j0, s0)
    s2 = scores(j2)
    finish(j1, s1)
    finish(j2, s2)


def _attention(za0, z2, z3):
    bsz, t, _ = za0.shape
    gw, nb = ATT_GROUP_W, ATT_BLOCK
    steps = 4
    rows1 = t // steps
    assert rows1 == 4 * nb and t // DILATIONS[1] == 4 * nb and t // DILATIONS[2] == nb
    prev = lambda b, j: (b, jnp.maximum(4 * j - 1, 0), 1)
    prev_v = lambda b, j: (b, jnp.maximum(4 * j - 1, 0), 2)
    in_specs = [
        pl.BlockSpec((1, rows1, gw), lambda b, j: (b, j, 0)),
        pl.BlockSpec((1, rows1, gw), lambda b, j: (b, j, 1)),
        pl.BlockSpec((1, rows1, gw), lambda b, j: (b, j, 2)),
        pl.BlockSpec((1, nb, gw), prev),
        pl.BlockSpec((1, nb, gw), prev_v),
        pl.BlockSpec((1, 4 * nb, gw), lambda b, j: (b, 0, 3 * j)),
        pl.BlockSpec((1, 4 * nb, gw), lambda b, j: (b, 0, 3 * j + 1)),
        pl.BlockSpec((1, 4 * nb, gw), lambda b, j: (b, 0, 3 * j + 2)),
        pl.BlockSpec((1, nb, 4 * 3 * gw), lambda b, j: (b, 0, j)),
    ]
    o1s = pl.BlockSpec((1, rows1, gw), lambda b, j: (b, j, 0))
    o2s = pl.BlockSpec((1, 4 * nb, gw), lambda b, j: (b, 0, j))
    o3s = pl.BlockSpec((1, nb, 4 * gw), lambda b, j: (b, 0, j))
    s1 = jax.ShapeDtypeStruct((bsz, t, gw), F32)
    s2 = jax.ShapeDtypeStruct((bsz, t // DILATIONS[1], DILATIONS[1] * gw), F32)
    s3 = jax.ShapeDtypeStruct((bsz, t // DILATIONS[2], DILATIONS[2] * gw), F32)
    o1, l1, o2, l2, o3, l3 = pl.pallas_call(
        _attn_kernel,
        grid=(bsz, steps),
        in_specs=in_specs,
        out_specs=[o1s, o1s, o2s, o2s, o3s, o3s],
        out_shape=[s1, s1, s2, s2, s3, s3],
        compiler_params=pltpu.CompilerParams(
            dimension_semantics=("arbitrary", "arbitrary"), vmem_limit_bytes=VMEM_LIMIT),
        name="dilated_attn",
    )(za0, za0, za0, za0, za0, z2, z2, z2, z3)
    return o1, l1, o2, l2, o3, l3


def _softplus(x):
    return jnp.maximum(x, 0.0) + jnp.log(1.0 + jnp.exp(-jnp.abs(x)))


def _chunk_masks():
    c = DN_CHUNK
    ii = lax.broadcasted_iota(jnp.int32, (c, c), 0)
    jj = lax.broadcasted_iota(jnp.int32, (c, c), 1)
    return ii >= jj, ii > jj, ii <= jj


def _chunk_decay(ab, abt, alog_ref, dtb_ref, alog_c_ref, dtb_c_ref, tril, triu):
    g_col = -jnp.exp(alog_ref[...]) * _softplus(ab + dtb_ref[...])
    g_row = -jnp.exp(alog_c_ref[...]) * _softplus(abt + dtb_c_ref[...])
    gc_col = _dot_exact(jnp.where(tril, 1.0, 0.0).astype(F32), g_col)
    gc_row = _dot_exact(g_row, jnp.where(triu, 1.0, 0.0).astype(F32))
    return gc_col, gc_row, _sigmoid(ab)


def _head_decay(gc_col, gc_row, beta_col, h, tril):
    c, hd = DN_CHUNK, DN_HEAD_DIM
    cb = jnp.broadcast_to(gc_col[:, h:h + 1], (c, hd))
    bb = jnp.broadcast_to(beta_col[:, DN_HEADS + h:DN_HEADS + h + 1], (c, hd))
    rb = jnp.broadcast_to(gc_row[h:h + 1, :], (c, c))
    ld = jnp.exp(jnp.where(tril, cb[:, :c] - rb, 0.0))
    return cb, bb, ld


def _dn_prep_kernel(k_ref, ab_ref, abt_ref, alog_ref, dtb_ref, alog_c_ref, dtb_c_ref, a_ref,
                    *, n_chunks):
    c, hd = DN_CHUNK, DN_HEAD_DIM
    tril, strict, triu = _chunk_masks()
    for n in range(n_chunks):
        r0 = n * c
        gc_col, gc_row, beta_col = _chunk_decay(ab_ref[0, r0:r0 + c, :], abt_ref[0, n], alog_ref,
                                                dtb_ref, alog_c_ref, dtb_c_ref, tril, triu)
        for h in range(DN_HEADS):
            k = k_ref[0, r0:r0 + c, h * hd:(h + 1) * hd]
            _, bb, ld = _head_decay(gc_col, gc_row, beta_col, h, tril)
            a_ref[0, n, h] = jnp.where(strict, _dot_nt(k, k) * (bb[:, :c] * ld), 0.0)


def _dn_solve_kernel(a_ref, t_ref, at_scr, tt_scr):
    c = DN_CHUNK
    tiles = c * c // LANE
    per = LANE // c
    for t in range(tiles):
        blk = a_ref[:, t * LANE:(t + 1) * LANE].T
        for r in range(per):
            at_scr[t * per + r] = blk[r * c:(r + 1) * c]
    row_id = lax.broadcasted_iota(jnp.int32, (SUBLANE, LANE), 0)
    zero = jnp.zeros((SUBLANE, LANE), F32)
    for i in range(c):
        groups = i // SUBLANE + 1
        acc = [zero] * (c // SUBLANE)
        acc[groups - 1] = jnp.where(row_id == i % SUBLANE, 1.0, 0.0).astype(F32)
        for j in range(i):
            a = jnp.broadcast_to(at_scr[i, j:j + 1, :], (SUBLANE, LANE))
            for g in range(j // SUBLANE + 1):
                acc[g] = acc[g] - a * tt_scr[j, g * SUBLANE:(g + 1) * SUBLANE, :]
        for g in range(c // SUBLANE):
            tt_scr[i, g * SUBLANE:(g + 1) * SUBLANE, :] = acc[g]
    for t in range(tiles):
        blk = jnp.concatenate([tt_scr[t * per + r] for r in range(per)], axis=0)
        t_ref[:, t * LANE:(t + 1) * LANE] = blk.T


def _dn_scan_kernel(q_ref, k_ref, v_ref, ab_ref, abt_ref, t_ref, alog_ref, dtb_ref,
                    alog_c_ref, dtb_c_ref, o_ref, s_ref, *, n_chunks):
    c, hd = DN_CHUNK, DN_HEAD_DIM

    @pl.when(pl.program_id(1) == 0)
    def _():
        s_ref[...] = jnp.zeros(s_ref.shape, F32)

    tril, _, triu = _chunk_masks()
    heads = range(DN_HEADS)

    def stage_a(n):
        r0 = n * c
        gc_col, gc_row, beta_col = _chunk_decay(ab_ref[0, r0:r0 + c, :], abt_ref[0, n], alog_ref,
                                                dtb_ref, alog_c_ref, dtb_c_ref, tril, triu)
        dec, qk, xs, egc = [], [], [], []
        for h in heads:
            dec.append(_head_decay(gc_col, gc_row, beta_col, h, tril))
        for h in heads:
            lanes = slice(h * hd, (h + 1) * hd)
            qk.append(_dot_nt(q_ref[0, r0:r0 + c, lanes], k_ref[0, r0:r0 + c, lanes]))
        for h in heads:
            lanes = slice(h * hd, (h + 1) * hd)
            cb, bb, _ = dec[h]
            eb = jnp.exp(cb)
            egc.append(eb.astype(BF16))
            rhs = jnp.concatenate([v_ref[0, r0:r0 + c, lanes] * bb.astype(BF16),
                                   k_ref[0, r0:r0 + c, lanes] * (bb * eb).astype(BF16)], axis=1)
            xs.append(_dot(t_ref[0, n, h].astype(BF16), rhs))
        per_head = []
        for h in heads:
            lanes = slice(h * hd, (h + 1) * hd)
            cb, _, ld = dec[h]
            k = k_ref[0, r0:r0 + c, lanes].astype(F32)
            gl = cb[c - 1:c, :]
            attn = jnp.where(tril, qk[h] * ld, 0.0).astype(BF16)
            kdec_t = (k * jnp.exp(gl - cb)).T.astype(BF16)
            lhs_s = jnp.concatenate([xs[h][:, hd:].astype(BF16),
                                     q_ref[0, r0:r0 + c, lanes] * egc[h]], axis=0)
            per_head.append((xs[h][:, :hd], lhs_s, jnp.concatenate([attn, kdec_t], axis=0),
                             jnp.exp(gl)))
        return per_head

    state = [s_ref[h] for h in heads]
    cur = stage_a(0)
    for n in range(n_chunks):
        r0 = n * c
        ws = [_dot(cur[h][1], state[h].astype(BF16)) for h in heads]
        nxt = stage_a(n + 1) if n + 1 < n_chunks else None
        vn = [(cur[h][0] - ws[h][:c]).astype(BF16) for h in heads]
        upd = [_dot(cur[h][2], vn[h]) for h in heads]
        for h in heads:
            lanes = slice(h * hd, (h + 1) * hd)
            o_ref[0, r0:r0 + c, lanes] = (ws[h][c:] + upd[h][:c]).astype(BF16)
            state[h] = state[h] * cur[h][3] + upd[h][c:]
        cur = nxt
    for h in heads:
        s_ref[h] = state[h]


def _deltanet(dq, dk, dv, ab, abt, a_log, dt_bias, rows=256):
    bsz, t, w = dq.shape
    c = DN_CHUNK
    n_chunks = rows // c
    total_chunks = t // c
    abt_c = abt.reshape(bsz, 2 * DN_HEADS, total_chunks, c).transpose(0, 2, 1, 3)
    pad = jnp.zeros((DN_HEADS,), F32)
    alog16 = jnp.concatenate([a_log, pad])
    dtb16 = jnp.concatenate([dt_bias, pad])
    small = [alog16.reshape(1, -1), dtb16.reshape(1, -1), alog16.reshape(-1, 1), dtb16.reshape(-1, 1)]
    small_specs = [_const_spec((1, 2 * DN_HEADS), single_buffer=False)] * 2 + \
                  [_const_spec((2 * DN_HEADS, 1), single_buffer=False)] * 2
    row = pl.BlockSpec((1, rows, w), lambda b, i: (b, i, 0))
    ab_spec = pl.BlockSpec((1, rows, 2 * DN_HEADS), lambda b, i: (b, i, 0))
    abt_spec = pl.BlockSpec((1, n_chunks, 2 * DN_HEADS, c), lambda b, i: (b, i, 0, 0))
    mat_spec = pl.BlockSpec((1, n_chunks, DN_HEADS, c, c), lambda b, i: (b, i, 0, 0, 0))
    mat_shape = jax.ShapeDtypeStruct((bsz, total_chunks, DN_HEADS, c, c), F32)
    params = pltpu.CompilerParams(dimension_semantics=("arbitrary", "arbitrary"),
                                  vmem_limit_bytes=VMEM_LIMIT)

    a_mat = pl.pallas_call(
        functools.partial(_dn_prep_kernel, n_chunks=n_chunks),
        grid=(bsz, t // rows),
        in_specs=[row, ab_spec, abt_spec] + small_specs,
        out_specs=mat_spec,
        out_shape=mat_shape,
        compiler_params=params,
        name="dn_prep",
    )(dk, ab, abt_c, *small)

    n_sys = bsz * total_chunks * DN_HEADS
    assert n_sys % LANE == 0
    flat_spec = pl.BlockSpec((LANE, c * c), lambda i: (i, 0))
    t_mat = pl.pallas_call(
        _dn_solve_kernel,
        grid=(n_sys // LANE,),
        in_specs=[flat_spec],
        out_specs=flat_spec,
        out_shape=jax.ShapeDtypeStruct((n_sys, c * c), F32),
        scratch_shapes=[pltpu.VMEM((c, c, LANE), F32), pltpu.VMEM((c, c, LANE), F32)],
        compiler_params=pltpu.CompilerParams(dimension_semantics=("arbitrary",),
                                             vmem_limit_bytes=VMEM_LIMIT),
        name="dn_solve",
    )(a_mat.reshape(n_sys, c * c)).reshape(mat_shape.shape)

    return pl.pallas_call(
        functools.partial(_dn_scan_kernel, n_chunks=n_chunks),
        grid=(bsz, t // rows),
        in_specs=[row, row, row, ab_spec, abt_spec, mat_spec] + small_specs,
        out_specs=row,
        out_shape=jax.ShapeDtypeStruct((bsz, t, w), BF16),
        scratch_shapes=[pltpu.VMEM((DN_HEADS, DN_HEAD_DIM, DN_HEAD_DIM), F32)],
        compiler_params=params,
        name="dn_scan",
    )(dq, dk, dv, ab, abt_c, t_mat, *small)


def _merge_kernel(o1_ref, l1_ref, o2_ref, l2_ref, o3_ref, l3_ref, odn_ref, gate_ref, nrm_ref,
                  mg_ref, x_ref, gt_ref, wpa_ref, wpd_ref, wo_ref, out_ref, nat_scr):
    d = D_MODEL
    tm = x_ref.shape[1]
    slabs = ATT_GROUP_W // LANE
    for a, (ref, dil) in enumerate(((o2_ref, DILATIONS[1]), (l2_ref, DILATIONS[1]),
                                    (o3_ref, DILATIONS[2]), (l3_ref, DILATIONS[2]))):
        for r in range(dil):
            for s in range(slabs):
                lo = r * ATT_GROUP_W + s * LANE
                nat_scr[a, s, pl.ds(r, tm // dil, stride=dil), :] = ref[0, :, lo:lo + LANE]
    halves = [slice(i * (tm // 2), (i + 1) * (tm // 2)) for i in range(2)]
    yd = []
    for r in halves:
        gated = []
        for h in range(DN_HEADS):
            lanes = slice(h * DN_HEAD_DIM, (h + 1) * DN_HEAD_DIM)
            o = odn_ref[0, r, lanes].astype(F32)
            ms = jnp.mean(o * o, axis=-1, keepdims=True)
            on = (o * lax.rsqrt(ms + EPS)) * nrm_ref[...]
            gated.append((on * _silu(gate_ref[0, r, lanes].astype(F32))).astype(BF16))
        yd.append(_dot(jnp.concatenate(gated, axis=1), wpd_ref[...]))
    ya = []
    for r in halves:
        ys = []
        for s in range(slabs):
            cols = slice(s * LANE, (s + 1) * LANE)
            l1, l2, l3 = l1_ref[0, r, cols], nat_scr[1, s, r, :], nat_scr[3, s, r, :]
            m = jnp.maximum(jnp.maximum(l1, l2), l3)
            e1, e2, e3 = jnp.exp(l1 - m), jnp.exp(l2 - m), jnp.exp(l3 - m)
            y = e1 * o1_ref[0, r, cols] + e2 * nat_scr[0, s, r, :] + e3 * nat_scr[2, s, r, :]
            ys.append((y / (e1 + e2 + e3)).astype(BF16))
        ya.append(_dot(jnp.concatenate(ys, axis=1), wpa_ref[...]))
    for i, r in enumerate(halves):
        mix = mg_ref[0, r, :d].astype(F32) * ya[i] + mg_ref[0, r, d:].astype(F32) * yd[i]
        out = _dot(mix.astype(BF16), wo_ref[...])
        out_ref[0, r] = x_ref[0, r] + gt_ref[0] * out


def _merge(att, odn, gate, dn_norm, mg, x, mod_l, wpa, wpd, wo, tm=512):
    bsz, t, d = x.shape
    row = lambda n: pl.BlockSpec((1, tm, n), lambda b, i: (b, i, 0))
    view = lambda dil: pl.BlockSpec((1, tm // dil, dil * ATT_GROUP_W), lambda b, i: (b, i, 0))
    att_specs = [row(ATT_GROUP_W)] * 2 + [view(DILATIONS[1])] * 2 + [view(DILATIONS[2])] * 2
    return pl.pallas_call(
        _merge_kernel,
        grid=(bsz, t // tm),
        in_specs=att_specs + [row(DN_W), row(DN_W), _const_spec((1, DN_HEAD_DIM), single_buffer=False),
                              row(2 * d), row(d), _mod_spec(5),
                              _const_spec(wpa.shape), _const_spec(wpd.shape),
                              _const_spec(wo.shape)],
        out_specs=row(d),
        out_shape=jax.ShapeDtypeStruct(x.shape, F32),
        scratch_shapes=[pltpu.VMEM((4, ATT_GROUP_W // LANE, tm, LANE), F32)],
        compiler_params=pltpu.CompilerParams(
            dimension_semantics=("arbitrary", "arbitrary"), vmem_limit_bytes=VMEM_LIMIT),
        name="merge_out",
    )(*att, odn, gate, dn_norm.reshape(1, -1), mg, x, mod_l, wpa, wpd, wo)


def _mixer(x, mod_l, gain, w_packed, ones_blk, qn4, kn4, conv_w, a_log, dt_bias, dn_norm,
           wpa, wpd, wo):
    za0, za1, za2, dq, dk, dv, gate, mg, ab, abt = _inproj(
        x, mod_l, gain, w_packed, ones_blk, qn4, kn4, conv_w)
    att = _attention(za0, za1, za2)
    odn = _deltanet(dq, dk, dv, ab, abt, a_log, dt_bias)
    return _merge(att, odn, gate, dn_norm, mg, x, mod_l, wpa, wpd, wo)


def kernel(x, c, ada_w, ada_b, norm_ff1, ffn1_w_up, ffn1_w_down, norm_mix, w_in, q_norm, k_norm,
           conv_w, a_log, dt_bias, dn_norm, w_proj_att, w_proj_dn, w_out, norm_ff2, ffn2_w_up,
           ffn2_w_down):
    bsz = x.shape[0]
    mod = _ada_mod(c, ada_w, ada_b)
    blk = jnp.arange(ATT_GROUP_W) // ATT_HEAD_DIM
    ones_blk = jnp.where(blk[:, None] == blk[None, :], 1.0 / ATT_HEAD_DIM, 0.0).astype(BF16)
    w_in_packed = _pack_w_in(w_in)
    for l in range(DEPTH):
        mod_l = mod[l].reshape(bsz, 1, N_ADA * D_MODEL)
        x = _ffn(x, mod_l, 0, norm_ff1[l], ffn1_w_up[l].astype(BF16), ffn1_w_down[l].astype(BF16))
        x = _mixer(x, mod_l, norm_mix[l], w_in_packed[l], ones_blk,
                   jnp.tile(q_norm[l], ATT_HEADS).reshape(1, -1),
                   jnp.tile(k_norm[l], ATT_HEADS).reshape(1, -1),
                   conv_w[l], a_log[l], dt_bias[l], dn_norm[l],
                   w_proj_att[l].astype(BF16), w_proj_dn[l].astype(BF16), w_out[l].astype(BF16))
        x = _ffn(x, mod_l, 6, norm_ff2[l], ffn2_w_up[l].astype(BF16), ffn2_w_down[l].astype(BF16))
    return x
```

```python
import functools

import jax
import jax.numpy as jnp
from jax import lax
from jax.experimental import pallas as pl
from jax.experimental.pallas import tpu as pltpu

F32 = jnp.float32
BF16 = jnp.bfloat16

D_MODEL = 1024
DEPTH = 4
SEQ = 2048
D_FF = 2816
EPS = 1e-6
N_ADA = 9

ATT_HEAD_DIM = 64
ATT_HEADS = 4
ATT_GROUP_W = ATT_HEADS * ATT_HEAD_DIM
DILATIONS = (1, 4, 16)
ATT_BLOCK = 128
N_GROUPS = 3
ATT_W = N_GROUPS * ATT_GROUP_W

DN_HEADS = 8
DN_HEAD_DIM = 128
DN_W = DN_HEADS * DN_HEAD_DIM
DN_CHUNK = 64
CONV_W = 4

OFF_DN_QKV = 3 * ATT_W
OFF_DN_GATE = OFF_DN_QKV + 3 * DN_W
OFF_DN_A = OFF_DN_GATE + DN_W
OFF_DN_B = OFF_DN_A + DN_HEADS
OFF_MERGE = OFF_DN_B + DN_HEADS
N_IN = OFF_MERGE + 2 * D_MODEL

P_ATT = 0
P_DN = P_ATT + 3 * ATT_W
P_GATE = P_DN + 3 * DN_W
P_MERGE = P_GATE + DN_W
P_AB = P_MERGE + 2 * D_MODEL
P_TOTAL = P_AB + 128

LANE = 128
SUBLANE = 8
NEG_BIG = -1e30
VMEM_LIMIT = 56 * 1024 * 1024
INPROJ_LOOKAHEAD = 1


def _dot(a, b):
    return jnp.dot(a, b, preferred_element_type=F32)


def _dot_nt(a, b):
    return lax.dot_general(a, b, (((1,), (1,)), ((), ())), preferred_element_type=F32)


def _dot_exact(a, b):
    return jnp.dot(a, b, preferred_element_type=F32, precision=lax.Precision.HIGHEST)


def _sigmoid(x):
    return 1.0 / (1.0 + jnp.exp(-x))


def _silu(x):
    return x * _sigmoid(x)


def _norm_mod(x, gain, shift, scale):
    ms = jnp.mean(x * x, axis=-1, keepdims=True)
    y = (x * lax.rsqrt(ms + EPS)) * gain
    return y * (1.0 + scale) + shift


def _const_spec(shape, single_buffer=True):
    nd = len(shape)
    kwargs = {"pipeline_mode": pl.Buffered(1)} if single_buffer else {}
    return pl.BlockSpec(shape, lambda *_: (0,) * nd, **kwargs)


def _layer_spec(shape, layer):
    nd = len(shape)
    return pl.BlockSpec((None,) + tuple(shape), lambda *_: (layer,) + (0,) * nd,
                        pipeline_mode=pl.Buffered(1))


def _mod_spec(k):
    return pl.BlockSpec((1, 1, D_MODEL), lambda b, i: (b, 0, k))


def _ada_kernel(c_ref, w_ref, b_ref, o_ref):
    c = c_ref[...]
    ca = _silu(c).astype(BF16)
    o_ref[0] = _dot(ca, w_ref[0].astype(BF16)) + b_ref[0]


def _ada_mod(c, ada_w, ada_b, tn=1024):
    depth, d, n = ada_w.shape
    bsz = c.shape[0]
    return pl.pallas_call(
        _ada_kernel,
        grid=(depth, n // tn),
        in_specs=[
            pl.BlockSpec((bsz, d), lambda l, j: (0, 0)),
            pl.BlockSpec((1, d, tn), lambda l, j: (l, 0, j)),
            pl.BlockSpec((1, 1, tn), lambda l, j: (l, 0, j)),
        ],
        out_specs=pl.BlockSpec((1, bsz, tn), lambda l, j: (l, 0, j)),
        out_shape=jax.ShapeDtypeStruct((depth, bsz, n), F32),
        name="ada_mod",
    )(c, ada_w, ada_b.reshape(depth, 1, n))


def _ffn_kernel(x_ref, g_ref, sh_ref, sc_ref, gt_ref, wup_ref, wdn_ref, o_ref, *, tf):
    x = x_ref[0]
    hb = _norm_mod(x, g_ref[...], sh_ref[0], sc_ref[0]).astype(BF16)
    acc = jnp.zeros(x.shape, F32)
    for f in range(D_FF // tf):
        gate = _dot(hb, wup_ref[:, f * tf:(f + 1) * tf])
        up = _dot(hb, wup_ref[:, D_FF + f * tf:D_FF + (f + 1) * tf])
        a = (_silu(gate) * up).astype(BF16)
        acc = acc + _dot(a, wdn_ref[f * tf:(f + 1) * tf, :])
    o_ref[0] = x + (0.5 * gt_ref[0]) * acc


def _ffn(x, mod_l, k0, gain, wup, wdn, layer, tm=512, tf=256):
    bsz, t, d = x.shape
    return pl.pallas_call(
        functools.partial(_ffn_kernel, tf=tf),
        grid=(bsz, t // tm),
        in_specs=[
            pl.BlockSpec((1, tm, d), lambda b, i: (b, i, 0)),
            _const_spec((1, d), single_buffer=False),
            _mod_spec(k0), _mod_spec(k0 + 1), _mod_spec(k0 + 2),
            _layer_spec((d, 2 * D_FF), layer),
            _layer_spec((D_FF, d), layer),
        ],
        out_specs=pl.BlockSpec((1, tm, d), lambda b, i: (b, i, 0)),
        out_shape=jax.ShapeDtypeStruct(x.shape, F32),
        compiler_params=pltpu.CompilerParams(
            dimension_semantics=("arbitrary", "arbitrary"), vmem_limit_bytes=VMEM_LIMIT),
        name="ffn",
    )(x, gain.reshape(1, d), mod_l, mod_l, mod_l, wup, wdn)


def _pack_w_in(w):
    q, k, v = w[..., :ATT_W], w[..., ATT_W:2 * ATT_W], w[..., 2 * ATT_W:3 * ATT_W]
    parts = []
    for g in range(N_GROUPS):
        sl = slice(g * ATT_GROUP_W, (g + 1) * ATT_GROUP_W)
        parts += [q[..., sl], k[..., sl], v[..., sl]]
    parts.append(w[..., OFF_DN_QKV:OFF_DN_A])
    parts.append(w[..., OFF_MERGE:])
    parts.append(w[..., OFF_DN_A:OFF_MERGE])
    parts.append(jnp.zeros(w.shape[:-1] + (128 - 2 * DN_HEADS,), w.dtype))
    return jnp.concatenate([p.astype(BF16) for p in parts], axis=-1)


def _inproj_kernel(x_ref, g_ref, sh_ref, sc_ref, w_ref, ones_ref, qn_ref, kn_ref, cw_ref,
                   za0_ref, za1_ref, za2_ref, dq_ref, dk_ref, dv_ref, gate_ref, mg_ref,
                   ab_ref, abt_ref, carry_ref, dscr_ref, *, tm, cn):
    x = x_ref[0]
    hb = _norm_mod(x, g_ref[...], sh_ref[0], sc_ref[0]).astype(BF16)

    @pl.when(pl.program_id(1) == 0)
    def _():
        carry_ref[...] = jnp.zeros(carry_ref.shape, F32)

    jobs = []
    gw = ATT_GROUP_W

    def att_store(za_ref, g, part, val):
        dil = DILATIONS[g]
        if dil == 1:
            za_ref[0, :, part * gw:(part + 1) * gw] = val.astype(BF16)
            return
        for s in range(gw // LANE):
            dscr_ref[s] = val[:, s * LANE:(s + 1) * LANE]
        for r in range(dil):
            for s in range(gw // LANE):
                lo = r * 3 * gw + part * gw + s * LANE
                za_ref[0, :, lo:lo + LANE] = dscr_ref[s, pl.ds(r, tm // dil, stride=dil), :].astype(BF16)

    def att_norm(za_ref, g, part, gain_ref, mult):
        def epilogue(z):
            ms = _dot((z * z).astype(BF16), ones_ref[...])
            att_store(za_ref, g, part, (z * lax.rsqrt(ms + EPS)) * (gain_ref[...] * mult))
        return epilogue

    def att_plain(za_ref, g):
        def epilogue(z):
            att_store(za_ref, g, 2, z)
        return epilogue

    light = []
    for g, za_ref in enumerate((za0_ref, za1_ref, za2_ref)):
        c0 = P_ATT + g * 3 * gw
        light.append((c0, gw, att_norm(za_ref, g, 0, qn_ref, ATT_HEAD_DIM ** -0.5)))
        light.append((c0 + gw, gw, att_norm(za_ref, g, 1, kn_ref, 1.0)))
        light.append((c0 + 2 * gw, gw, att_plain(za_ref, g)))

    def dn_conv(out_ref, part, j):
        dc = part * DN_W + j * cn

        def epilogue(z):
            xe = jnp.concatenate([carry_ref[:, dc:dc + cn], z], axis=0)
            carry_ref[:, dc:dc + cn] = z[tm - SUBLANE:tm, :]
            y = cw_ref[CONV_W - 1:CONV_W, dc:dc + cn] * z
            for s in range(1, CONV_W):
                shifted = pltpu.roll(xe, s, axis=0)[SUBLANE:, :]
                y = y + cw_ref[CONV_W - 1 - s:CONV_W - s, dc:dc + cn] * shifted
            y = _silu(y)
            if part < 2:
                mult = DN_HEAD_DIM ** -0.5 if part == 0 else 1.0
                for h in range(cn // DN_HEAD_DIM):
                    yh = y[:, h * DN_HEAD_DIM:(h + 1) * DN_HEAD_DIM]
                    ss = jnp.sum(yh * yh, axis=-1, keepdims=True)
                    yn = yh * (lax.rsqrt(ss + EPS) * mult)
                    lo = j * cn + h * DN_HEAD_DIM
                    out_ref[0, :, lo:lo + DN_HEAD_DIM] = yn.astype(BF16)
            else:
                out_ref[0, :, j * cn:(j + 1) * cn] = y.astype(BF16)
        return epilogue

    heavy = []
    for part, out_ref in enumerate((dq_ref, dk_ref, dv_ref)):
        for j in range(DN_W // cn):
            heavy.append((P_DN + part * DN_W + j * cn, cn, dn_conv(out_ref, part, j)))

    def store_cast(out_ref, j, fn):
        def epilogue(z):
            out_ref[0, :, j * cn:(j + 1) * cn] = fn(z).astype(BF16)
        return epilogue

    for j in range(DN_W // cn):
        light.append((P_GATE + j * cn, cn, store_cast(gate_ref, j, lambda z: z)))
    for j in range(2 * D_MODEL // cn):
        light.append((P_MERGE + j * cn, cn, store_cast(mg_ref, j, _sigmoid)))

    def ab_epilogue(z):
        ab_ref[0] = z[:, :2 * DN_HEADS]
        abt_ref[0] = z.T[:2 * DN_HEADS, :]

    light.append((P_AB, 128, ab_epilogue))

    per_heavy = -(-len(light) // len(heavy))
    for i, job in enumerate(heavy):
        jobs.append(job)
        jobs.extend(light[i * per_heavy:(i + 1) * per_heavy])

    def project(job):
        return _dot(hb, w_ref[:, job[0]:job[0] + job[1]])

    pending = [project(job) for job in jobs[:INPROJ_LOOKAHEAD]]
    for i, job in enumerate(jobs):
        if i + INPROJ_LOOKAHEAD < len(jobs):
            pending.append(project(jobs[i + INPROJ_LOOKAHEAD]))
        job[2](pending.pop(0))


def _inproj(x, mod_l, gain, w_packed, layer, ones_blk, qn4, kn4, conv_w, tm=512, cn=512):
    bsz, t, d = x.shape
    bf = lambda n: jax.ShapeDtypeStruct((bsz, t, n), BF16)
    row = lambda n: pl.BlockSpec((1, tm, n), lambda b, i: (b, i, 0))
    att_spec = lambda dil: pl.BlockSpec((1, tm // dil, dil * 3 * ATT_GROUP_W), lambda b, i: (b, i, 0))
    att_shape = lambda dil: jax.ShapeDtypeStruct((bsz, t // dil, dil * 3 * ATT_GROUP_W), BF16)
    return pl.pallas_call(
        functools.partial(_inproj_kernel, tm=tm, cn=cn),
        grid=(bsz, t // tm),
        in_specs=[
            row(d),
            _const_spec((1, d), single_buffer=False),
            _mod_spec(3), _mod_spec(4),
            _layer_spec((d, P_TOTAL), layer),
            _const_spec((ATT_GROUP_W, ATT_GROUP_W)),
            _const_spec((1, ATT_GROUP_W), single_buffer=False),
            _const_spec((1, ATT_GROUP_W), single_buffer=False),
            _const_spec((CONV_W, 3 * DN_W), single_buffer=False),
        ],
        out_specs=[att_spec(dil) for dil in DILATIONS] + [row(DN_W)] * 4 + [row(2 * D_MODEL)]
        + [row(2 * DN_HEADS), pl.BlockSpec((1, 2 * DN_HEADS, tm), lambda b, i: (b, 0, i))],
        out_shape=[att_shape(dil) for dil in DILATIONS] + [bf(DN_W)] * 4 + [bf(2 * D_MODEL)]
        + [jax.ShapeDtypeStruct((bsz, t, 2 * DN_HEADS), F32),
           jax.ShapeDtypeStruct((bsz, 2 * DN_HEADS, t), F32)],
        scratch_shapes=[pltpu.VMEM((SUBLANE, 3 * DN_W), F32),
                        pltpu.VMEM((ATT_GROUP_W // LANE, tm, LANE), F32)],
        compiler_params=pltpu.CompilerParams(
            dimension_semantics=("arbitrary", "arbitrary"), vmem_limit_bytes=VMEM_LIMIT),
        name="inproj",
    )(x, gain.reshape(1, d), mod_l, mod_l, w_packed, ones_blk, qn4, kn4, conv_w)


def _attn_scores(q, kk, bias, head_masks):
    zero = jnp.zeros_like(q)
    qs = jnp.concatenate([jnp.where(hm, q, zero) for hm in head_masks], axis=0)
    return _dot_nt(qs, kk) + jnp.concatenate([bias] * ATT_HEADS, axis=0)


def _attn_probs(s):
    m = jnp.max(s, axis=-1, keepdims=True)
    p = jnp.exp(s - m)
    l = jnp.sum(p, axis=-1, keepdims=True)
    return p.astype(BF16), 1.0 / l, m + jnp.log(l)


def _attn_out(pv, inv, lse, head_masks):
    nq = pv.shape[0] // ATT_HEADS
    o = jnp.zeros((nq, ATT_GROUP_W), F32)
    ls = jnp.zeros((nq, ATT_GROUP_W), F32)
    for h, hm in enumerate(head_masks):
        rows = slice(h * nq, (h + 1) * nq)
        o = jnp.where(hm, pv[rows] * inv[rows], o)
        ls = jnp.where(hm, jnp.broadcast_to(lse[rows], (nq, ATT_GROUP_W)), ls)
    return o, ls


def _attn_kernel(q1_ref, k1_ref, v1_ref, k1p_ref, v1p_ref, q2_ref, k2_ref, v2_ref, z3_ref,
                 o1_ref, l1_ref, o2_ref, l2_ref, o3_ref, l3_ref):
    nb = ATT_BLOCK
    gw = ATT_GROUP_W
    step = pl.program_id(1)
    lane = lax.broadcasted_iota(jnp.int32, (1, gw), 1)
    head_masks = [(lane // ATT_HEAD_DIM) == h for h in range(ATT_HEADS)]
    qi = lax.broadcasted_iota(jnp.int32, (nb, 2 * nb), 0)
    kj = lax.broadcasted_iota(jnp.int32, (nb, 2 * nb), 1)
    ok_two = (kj >= qi) & (kj <= qi + nb)
    ok_cur = (kj >= nb) & (kj <= qi + nb)
    bias_two = jnp.where(ok_two, 0.0, NEG_BIG).astype(F32)
    bias_first = jnp.where(ok_cur, 0.0, NEG_BIG).astype(F32)
    bias_one = bias_first[:, nb:]

    def blocks(i):
        return slice(i * nb, (i + 1) * nb), slice((i - 1) * nb, (i + 1) * nb)

    def group0():
        jobs = []
        for i in range(4):
            rows, keys = blocks(i)
            if i == 0:
                kk = jnp.concatenate([k1p_ref[0], k1_ref[0, rows]], axis=0)
                vv = jnp.concatenate([v1p_ref[0], v1_ref[0, rows]], axis=0)
                bias = jnp.where(step == 0, bias_first, bias_two)
            else:
                kk, vv, bias = k1_ref[0, keys], v1_ref[0, keys], bias_two
            jobs.append((q1_ref[0, rows], kk, vv, bias, (o1_ref, l1_ref, rows, slice(None))))
        return jobs

    def group1():
        jobs = []
        for i in range(4):
            rows, keys = blocks(i)
            if i == 0:
                kk, vv, bias = k2_ref[0, rows], v2_ref[0, rows], bias_one
            else:
                kk, vv, bias = k2_ref[0, keys], v2_ref[0, keys], bias_two
            jobs.append((q2_ref[0, rows], kk, vv, bias, (o2_ref, l2_ref, rows, slice(None))))
        return jobs

    def group2():
        jobs = []
        for r in range(4):
            c0 = r * 3 * gw
            jobs.append((z3_ref[0, :, c0:c0 + gw], z3_ref[0, :, c0 + gw:c0 + 2 * gw],
                         z3_ref[0, :, c0 + 2 * gw:c0 + 3 * gw], bias_one,
                         (o3_ref, l3_ref, slice(None), slice(r * gw, (r + 1) * gw))))
        return jobs

    def scores(jobs):
        return [_attn_scores(q, kk, bias, head_masks) for q, kk, _, bias, _ in jobs]

    def finish(jobs, ss):
        probs = [_attn_probs(s) for s in ss]
        pvs = [_dot(p, job[2]) for (p, _, _), job in zip(probs, jobs)]
        for pv, (_, inv, lse), job in zip(pvs, probs, jobs):
            o_ref, l_ref, rows, cols = job[4]
            o, ls = _attn_out(pv, inv, lse, head_masks)
            o_ref[0, rows, cols] = o
            l_ref[0, rows, cols] = ls

    j0, j1, j2 = group0(), group1(), group2()
    s0 = scores(j0)
    s1 = scores(j1)
    finish(j0, s0)
    s2 = scores(j2)
    finish(j1, s1)
    finish(j2, s2)


def _attention(za0, z2, z3):
    bsz, t, _ = za0.shape
    gw, nb = ATT_GROUP_W, ATT_BLOCK
    steps = 4
    rows1 = t // steps
    assert rows1 == 4 * nb and t // DILATIONS[1] == 4 * nb and t // DILATIONS[2] == nb
    prev = lambda b, j: (b, jnp.maximum(4 * j - 1, 0), 1)
    prev_v = lambda b, j: (b, jnp.maximum(4 * j - 1, 0), 2)
    in_specs = [
        pl.BlockSpec((1, rows1, gw), lambda b, j: (b, j, 0)),
        pl.BlockSpec((1, rows1, gw), lambda b, j: (b, j, 1)),
        pl.BlockSpec((1, rows1, gw), lambda b, j: (b, j, 2)),
        pl.BlockSpec((1, nb, gw), prev),
        pl.BlockSpec((1, nb, gw), prev_v),
        pl.BlockSpec((1, 4 * nb, gw), lambda b, j: (b, 0, 3 * j)),
        pl.BlockSpec((1, 4 * nb, gw), lambda b, j: (b, 0, 3 * j + 1)),
        pl.BlockSpec((1, 4 * nb, gw), lambda b, j: (b, 0, 3 * j + 2)),
        pl.BlockSpec((1, nb, 4 * 3 * gw), lambda b, j: (b, 0, j)),
    ]
    o1s = pl.BlockSpec((1, rows1, gw), lambda b, j: (b, j, 0))
    o2s = pl.BlockSpec((1, 4 * nb, gw), lambda b, j: (b, 0, j))
    o3s = pl.BlockSpec((1, nb, 4 * gw), lambda b, j: (b, 0, j))
    s1 = jax.ShapeDtypeStruct((bsz, t, gw), F32)
    s2 = jax.ShapeDtypeStruct((bsz, t // DILATIONS[1], DILATIONS[1] * gw), F32)
    s3 = jax.ShapeDtypeStruct((bsz, t // DILATIONS[2], DILATIONS[2] * gw), F32)
    o1, l1, o2, l2, o3, l3 = pl.pallas_call(
        _attn_kernel,
        grid=(bsz, steps),
        in_specs=in_specs,
        out_specs=[o1s, o1s, o2s, o2s, o3s, o3s],
        out_shape=[s1, s1, s2, s2, s3, s3],
        compiler_params=pltpu.CompilerParams(
            dimension_semantics=("arbitrary", "arbitrary"), vmem_limit_bytes=VMEM_LIMIT),
        name="dilated_attn",
    )(za0, za0, za0, za0, za0, z2, z2, z2, z3)
    return o1, l1, o2, l2, o3, l3


def _softplus(x):
    return jnp.maximum(x, 0.0) + jnp.log(1.0 + jnp.exp(-jnp.abs(x)))


def _chunk_masks():
    c = DN_CHUNK
    ii = lax.broadcasted_iota(jnp.int32, (c, c), 0)
    jj = lax.broadcasted_iota(jnp.int32, (c, c), 1)
    return ii >= jj, ii > jj, ii <= jj


def _chunk_decay(ab, abt, alog_ref, dtb_ref, alog_c_ref, dtb_c_ref, tril, triu):
    g_col = -jnp.exp(alog_ref[...]) * _softplus(ab + dtb_ref[...])
    g_row = -jnp.exp(alog_c_ref[...]) * _softplus(abt + dtb_c_ref[...])
    gc_col = _dot_exact(jnp.where(tril, 1.0, 0.0).astype(F32), g_col)
    gc_row = _dot_exact(g_row, jnp.where(triu, 1.0, 0.0).astype(F32))
    return gc_col, gc_row, _sigmoid(ab)


def _head_decay(gc_col, gc_row, beta_col, h, tril):
    c, hd = DN_CHUNK, DN_HEAD_DIM
    cb = jnp.broadcast_to(gc_col[:, h:h + 1], (c, hd))
    bb = jnp.broadcast_to(beta_col[:, DN_HEADS + h:DN_HEADS + h + 1], (c, hd))
    rb = jnp.broadcast_to(gc_row[h:h + 1, :], (c, c))
    ld = jnp.exp(jnp.where(tril, cb[:, :c] - rb, 0.0))
    return cb, bb, ld


def _dn_prep_kernel(k_ref, ab_ref, abt_ref, alog_ref, dtb_ref, alog_c_ref, dtb_c_ref, a_ref,
                    gcc_ref, gcr_ref, *, n_chunks):
    c, hd = DN_CHUNK, DN_HEAD_DIM
    tril, strict, triu = _chunk_masks()
    lane16 = lax.broadcasted_iota(jnp.int32, (c, 2 * DN_HEADS), 1)
    for n in range(n_chunks):
        r0 = n * c
        gc_col, gc_row, beta_col = _chunk_decay(ab_ref[0, r0:r0 + c, :], abt_ref[0, n], alog_ref,
                                                dtb_ref, alog_c_ref, dtb_c_ref, tril, triu)
        gcc_ref[0, r0:r0 + c, :] = jnp.where(lane16 < DN_HEADS, gc_col, beta_col)
        gcr_ref[0, n] = gc_row
        for h in range(DN_HEADS):
            k = k_ref[0, r0:r0 + c, h * hd:(h + 1) * hd]
            _, bb, ld = _head_decay(gc_col, gc_row, beta_col, h, tril)
            a_ref[0, n, h] = jnp.where(strict, _dot_nt(k, k) * (bb[:, :c] * ld), 0.0)


def _dn_solve_kernel(a_ref, t_ref, at_scr, tt_scr):
    c = DN_CHUNK
    tiles = c * c // LANE
    per = LANE // c
    for t in range(tiles):
        blk = a_ref[:, t * LANE:(t + 1) * LANE].T
        for r in range(per):
            at_scr[t * per + r] = blk[r * c:(r + 1) * c]
    row_id = lax.broadcasted_iota(jnp.int32, (SUBLANE, LANE), 0)
    zero = jnp.zeros((SUBLANE, LANE), F32)
    for i in range(c):
        groups = i // SUBLANE + 1
        acc = [zero] * (c // SUBLANE)
        acc[groups - 1] = jnp.where(row_id == i % SUBLANE, 1.0, 0.0).astype(F32)
        for j in range(i):
            a = jnp.broadcast_to(at_scr[i, j:j + 1, :], (SUBLANE, LANE))
            for g in range(j // SUBLANE + 1):
                acc[g] = acc[g] - a * tt_scr[j, g * SUBLANE:(g + 1) * SUBLANE, :]
        for g in range(c // SUBLANE):
            tt_scr[i, g * SUBLANE:(g + 1) * SUBLANE, :] = acc[g]
    for t in range(tiles):
        blk = jnp.concatenate([tt_scr[t * per + r] for r in range(per)], axis=0)
        t_ref[:, t * LANE:(t + 1) * LANE] = blk.T.astype(t_ref.dtype)


def _dn_scan_kernel(q_ref, k_ref, v_ref, gcc_ref, gcr_ref, t_ref, o_ref, s_ref, *, n_chunks):
    c, hd = DN_CHUNK, DN_HEAD_DIM

    @pl.when(pl.program_id(1) == 0)
    def _():
        s_ref[...] = jnp.zeros(s_ref.shape, F32)

    tril, _, _ = _chunk_masks()
    heads = range(DN_HEADS)

    def stage_a(n):
        r0 = n * c
        gc_col = beta_col = gcc_ref[0, r0:r0 + c, :]
        gc_row = gcr_ref[0, n]
        dec, qk, xs, egc = [], [], [], []
        for h in heads:
            dec.append(_head_decay(gc_col, gc_row, beta_col, h, tril))
        for h in heads:
            lanes = slice(h * hd, (h + 1) * hd)
            qk.append(_dot_nt(q_ref[0, r0:r0 + c, lanes], k_ref[0, r0:r0 + c, lanes]))
        for h in heads:
            lanes = slice(h * hd, (h + 1) * hd)
            cb, bb, _ = dec[h]
            eb = jnp.exp(cb)
            egc.append(eb.astype(BF16))
            rhs = jnp.concatenate([v_ref[0, r0:r0 + c, lanes] * bb.astype(BF16),
                                   k_ref[0, r0:r0 + c, lanes] * (bb * eb).astype(BF16)], axis=1)
            xs.append(_dot(t_ref[0, n, h], rhs))
        per_head = []
        for h in heads:
            lanes = slice(h * hd, (h + 1) * hd)
            cb, _, ld = dec[h]
            k = k_ref[0, r0:r0 + c, lanes].astype(F32)
            gl = cb[c - 1:c, :]
            attn = jnp.where(tril, qk[h] * ld, 0.0).astype(BF16)
            kdec_t = (k * jnp.exp(gl - cb)).T.astype(BF16)
            lhs_s = jnp.concatenate([xs[h][:, hd:].astype(BF16),
                                     q_ref[0, r0:r0 + c, lanes] * egc[h]], axis=0)
            per_head.append((xs[h][:, :hd], lhs_s, jnp.concatenate([attn, kdec_t], axis=0),
                             jnp.exp(gl)))
        return per_head

    state = [s_ref[h] for h in heads]
    cur = stage_a(0)
    for n in range(n_chunks):
        r0 = n * c
        ws = [_dot(cur[h][1], state[h].astype(BF16)) for h in heads]
        nxt = stage_a(n + 1) if n + 1 < n_chunks else None
        vn = [(cur[h][0] - ws[h][:c]).astype(BF16) for h in heads]
        upd = [_dot(cur[h][2], vn[h]) for h in heads]
        for h in heads:
            lanes = slice(h * hd, (h + 1) * hd)
            o_ref[0, r0:r0 + c, lanes] = (ws[h][c:] + upd[h][:c]).astype(BF16)
            state[h] = state[h] * cur[h][3] + upd[h][c:]
        cur = nxt
    for h in heads:
        s_ref[h] = state[h]


def _deltanet(dq, dk, dv, ab, abt, a_log, dt_bias, rows=256):
    bsz, t, w = dq.shape
    c = DN_CHUNK
    n_chunks = rows // c
    total_chunks = t // c
    abt_c = abt.reshape(bsz, 2 * DN_HEADS, total_chunks, c).transpose(0, 2, 1, 3)
    pad = jnp.zeros((DN_HEADS,), F32)
    alog16 = jnp.concatenate([a_log, pad])
    dtb16 = jnp.concatenate([dt_bias, pad])
    small = [alog16.reshape(1, -1), dtb16.reshape(1, -1), alog16.reshape(-1, 1), dtb16.reshape(-1, 1)]
    small_specs = [_const_spec((1, 2 * DN_HEADS), single_buffer=False)] * 2 + \
                  [_const_spec((2 * DN_HEADS, 1), single_buffer=False)] * 2
    row = pl.BlockSpec((1, rows, w), lambda b, i: (b, i, 0))
    ab_spec = pl.BlockSpec((1, rows, 2 * DN_HEADS), lambda b, i: (b, i, 0))
    abt_spec = pl.BlockSpec((1, n_chunks, 2 * DN_HEADS, c), lambda b, i: (b, i, 0, 0))
    mat_spec = pl.BlockSpec((1, n_chunks, DN_HEADS, c, c), lambda b, i: (b, i, 0, 0, 0))
    mat_shape = jax.ShapeDtypeStruct((bsz, total_chunks, DN_HEADS, c, c), F32)
    params = pltpu.CompilerParams(dimension_semantics=("arbitrary", "arbitrary"),
                                  vmem_limit_bytes=VMEM_LIMIT)

    a_mat, gcc, gcr = pl.pallas_call(
        functools.partial(_dn_prep_kernel, n_chunks=n_chunks),
        grid=(bsz, t // rows),
        in_specs=[row, ab_spec, abt_spec] + small_specs,
        out_specs=[mat_spec, ab_spec, abt_spec],
        out_shape=[mat_shape, jax.ShapeDtypeStruct(ab.shape, F32),
                   jax.ShapeDtypeStruct(abt_c.shape, F32)],
        compiler_params=params,
        name="dn_prep",
    )(dk, ab, abt_c, *small)

    n_sys = bsz * total_chunks * DN_HEADS
    assert n_sys % LANE == 0
    flat_spec = pl.BlockSpec((LANE, c * c), lambda i: (i, 0))
    t_mat = pl.pallas_call(
        _dn_solve_kernel,
        grid=(n_sys // LANE,),
        in_specs=[flat_spec],
        out_specs=flat_spec,
        out_shape=jax.ShapeDtypeStruct((n_sys, c * c), BF16),
        scratch_shapes=[pltpu.VMEM((c, c, LANE), F32), pltpu.VMEM((c, c, LANE), F32)],
        compiler_params=pltpu.CompilerParams(dimension_semantics=("arbitrary",),
                                             vmem_limit_bytes=VMEM_LIMIT),
        name="dn_solve",
    )(a_mat.reshape(n_sys, c * c)).reshape(mat_shape.shape)

    return pl.pallas_call(
        functools.partial(_dn_scan_kernel, n_chunks=n_chunks),
        grid=(bsz, t // rows),
        in_specs=[row, row, row, ab_spec, abt_spec, mat_spec],
        out_specs=row,
        out_shape=jax.ShapeDtypeStruct((bsz, t, w), BF16),
        scratch_shapes=[pltpu.VMEM((DN_HEADS, DN_HEAD_DIM, DN_HEAD_DIM), F32)],
        compiler_params=params,
        name="dn_scan",
    )(dq, dk, dv, gcc, gcr, t_mat)


def _merge_kernel(o1_ref, l1_ref, o2_ref, l2_ref, o3_ref, l3_ref, odn_ref, gate_ref, nrm_ref,
                  mg_ref, x_ref, gt_ref, wpa_ref, wpd_ref, wo_ref, out_ref, nat_scr):
    d = D_MODEL
    tm = x_ref.shape[1]
    slabs = ATT_GROUP_W // LANE
    for a, (ref, dil) in enumerate(((o2_ref, DILATIONS[1]), (l2_ref, DILATIONS[1]),
                                    (o3_ref, DILATIONS[2]), (l3_ref, DILATIONS[2]))):
        for r in range(dil):
            for s in range(slabs):
                lo = r * ATT_GROUP_W + s * LANE
                nat_scr[a, s, pl.ds(r, tm // dil, stride=dil), :] = ref[0, :, lo:lo + LANE]
    halves = [slice(i * (tm // 2), (i + 1) * (tm // 2)) for i in range(2)]
    yd = []
    for r in halves:
        gated = []
        for h in range(DN_HEADS):
            lanes = slice(h * DN_HEAD_DIM, (h + 1) * DN_HEAD_DIM)
            o = odn_ref[0, r, lanes].astype(F32)
            ms = jnp.mean(o * o, axis=-1, keepdims=True)
            on = (o * lax.rsqrt(ms + EPS)) * nrm_ref[...]
            gated.append((on * _silu(gate_ref[0, r, lanes].astype(F32))).astype(BF16))
        yd.append(_dot(jnp.concatenate(gated, axis=1), wpd_ref[...]))
    ya = []
    for r in halves:
        ys = []
        for s in range(slabs):
            cols = slice(s * LANE, (s + 1) * LANE)
            l1, l2, l3 = l1_ref[0, r, cols], nat_scr[1, s, r, :], nat_scr[3, s, r, :]
            m = jnp.maximum(jnp.maximum(l1, l2), l3)
            e1, e2, e3 = jnp.exp(l1 - m), jnp.exp(l2 - m), jnp.exp(l3 - m)
            y = e1 * o1_ref[0, r, cols] + e2 * nat_scr[0, s, r, :] + e3 * nat_scr[2, s, r, :]
            ys.append((y / (e1 + e2 + e3)).astype(BF16))
        ya.append(_dot(jnp.concatenate(ys, axis=1), wpa_ref[...]))
    for i, r in enumerate(halves):
        mix = mg_ref[0, r, :d].astype(F32) * ya[i] + mg_ref[0, r, d:].astype(F32) * yd[i]
        out = _dot(mix.astype(BF16), wo_ref[...])
        out_ref[0, r] = x_ref[0, r] + gt_ref[0] * out


def _merge(att, odn, gate, dn_norm, mg, x, mod_l, wpa, wpd, wo, layer, tm=512):
    bsz, t, d = x.shape
    row = lambda n: pl.BlockSpec((1, tm, n), lambda b, i: (b, i, 0))
    view = lambda dil: pl.BlockSpec((1, tm // dil, dil * ATT_GROUP_W), lambda b, i: (b, i, 0))
    att_specs = [row(ATT_GROUP_W)] * 2 + [view(DILATIONS[1])] * 2 + [view(DILATIONS[2])] * 2
    return pl.pallas_call(
        _merge_kernel,
        grid=(bsz, t // tm),
        in_specs=att_specs + [row(DN_W), row(DN_W), _const_spec((1, DN_HEAD_DIM), single_buffer=False),
                              row(2 * d), row(d), _mod_spec(5),
                              _layer_spec(wpa.shape[1:], layer), _layer_spec(wpd.shape[1:], layer),
                              _layer_spec(wo.shape[1:], layer)],
        out_specs=row(d),
        out_shape=jax.ShapeDtypeStruct(x.shape, F32),
        scratch_shapes=[pltpu.VMEM((4, ATT_GROUP_W // LANE, tm, LANE), F32)],
        compiler_params=pltpu.CompilerParams(
            dimension_semantics=("arbitrary", "arbitrary"), vmem_limit_bytes=VMEM_LIMIT),
        name="merge_out",
    )(*att, odn, gate, dn_norm.reshape(1, -1), mg, x, mod_l, wpa, wpd, wo)


def _mixer(x, mod_l, layer, gain, w_packed, ones_blk, qn4, kn4, conv_w, a_log, dt_bias, dn_norm,
           wpa, wpd, wo):
    za0, za1, za2, dq, dk, dv, gate, mg, ab, abt = _inproj(
        x, mod_l, gain, w_packed, layer, ones_blk, qn4, kn4, conv_w)
    att = _attention(za0, za1, za2)
    odn = _deltanet(dq, dk, dv, ab, abt, a_log, dt_bias)
    return _merge(att, odn, gate, dn_norm, mg, x, mod_l, wpa, wpd, wo, layer)


def kernel(x, c, ada_w, ada_b, norm_ff1, ffn1_w_up, ffn1_w_down, norm_mix, w_in, q_norm, k_norm,
           conv_w, a_log, dt_bias, dn_norm, w_proj_att, w_proj_dn, w_out, norm_ff2, ffn2_w_up,
           ffn2_w_down):
    bsz = x.shape[0]
    mod = _ada_mod(c, ada_w, ada_b)
    blk = jnp.arange(ATT_GROUP_W) // ATT_HEAD_DIM
    ones_blk = jnp.where(blk[:, None] == blk[None, :], 1.0 / ATT_HEAD_DIM, 0.0).astype(BF16)
    w_in_packed = _pack_w_in(w_in)
    up1, dn1, up2, dn2 = (w.astype(BF16) for w in (ffn1_w_up, ffn1_w_down, ffn2_w_up, ffn2_w_down))
    wpa, wpd, wo = (w.astype(BF16) for w in (w_proj_att, w_proj_dn, w_out))
    for l in range(DEPTH):
        mod_l = mod[l].reshape(bsz, 1, N_ADA * D_MODEL)
        x = _ffn(x, mod_l, 0, norm_ff1[l], up1, dn1, l)
        x = _mixer(x, mod_l, l, norm_mix[l], w_in_packed, ones_blk,
                   jnp.tile(q_norm[l], ATT_HEADS).reshape(1, -1),
                   jnp.tile(k_norm[l], ATT_HEADS).reshape(1, -1),
                   conv_w[l], a_log[l], dt_bias[l], dn_norm[l], wpa, wpd, wo)
        x = _ffn(x, mod_l, 6, norm_ff2[l], up2, dn2, l)
    return x
```

```python
import functools

import jax
import jax.numpy as jnp
from jax import lax
from jax.experimental import pallas as pl
from jax.experimental.pallas import tpu as pltpu

F32 = jnp.float32
BF16 = jnp.bfloat16

D_MODEL = 1024
DEPTH = 4
SEQ = 2048
D_FF = 2816
EPS = 1e-6
N_ADA = 9

ATT_HEAD_DIM = 64
ATT_HEADS = 4
ATT_GROUP_W = ATT_HEADS * ATT_HEAD_DIM
DILATIONS = (1, 4, 16)
ATT_BLOCK = 128
N_GROUPS = 3
ATT_W = N_GROUPS * ATT_GROUP_W

DN_HEADS = 8
DN_HEAD_DIM = 128
DN_W = DN_HEADS * DN_HEAD_DIM
DN_CHUNK = 64
CONV_W = 4

OFF_DN_QKV = 3 * ATT_W
OFF_DN_GATE = OFF_DN_QKV + 3 * DN_W
OFF_DN_A = OFF_DN_GATE + DN_W
OFF_DN_B = OFF_DN_A + DN_HEADS
OFF_MERGE = OFF_DN_B + DN_HEADS
N_IN = OFF_MERGE + 2 * D_MODEL

P_ATT = 0
P_DN = P_ATT + 3 * ATT_W
P_GATE = P_DN + 3 * DN_W
P_MERGE = P_GATE + DN_W
P_AB = P_MERGE + 2 * D_MODEL
P_TOTAL = P_AB + 128

LANE = 128
SUBLANE = 8
NEG_BIG = -1e30
VMEM_LIMIT = 56 * 1024 * 1024
INPROJ_LOOKAHEAD = 1


def _dot(a, b):
    return jnp.dot(a, b, preferred_element_type=F32)


def _dot_nt(a, b):
    return lax.dot_general(a, b, (((1,), (1,)), ((), ())), preferred_element_type=F32)


def _dot_exact(a, b):
    return jnp.dot(a, b, preferred_element_type=F32, precision=lax.Precision.HIGHEST)


def _sigmoid(x):
    return 1.0 / (1.0 + jnp.exp(-x))


def _silu(x):
    return x * _sigmoid(x)


def _norm_mod(x, gain, shift, scale):
    ms = jnp.mean(x * x, axis=-1, keepdims=True)
    y = (x * lax.rsqrt(ms + EPS)) * gain
    return y * (1.0 + scale) + shift


def _const_spec(shape, single_buffer=True):
    nd = len(shape)
    kwargs = {"pipeline_mode": pl.Buffered(1)} if single_buffer else {}
    return pl.BlockSpec(shape, lambda *_: (0,) * nd, **kwargs)


def _layer_spec(shape, layer):
    nd = len(shape)
    return pl.BlockSpec((None,) + tuple(shape), lambda *_: (layer,) + (0,) * nd,
                        pipeline_mode=pl.Buffered(1))


def _mod_spec(k):
    return pl.BlockSpec((1, 1, D_MODEL), lambda b, i: (b, 0, k))


def _ada_kernel(c_ref, w_ref, b_ref, o_ref):
    c = c_ref[...]
    ca = _silu(c).astype(BF16)
    o_ref[0] = _dot(ca, w_ref[0].astype(BF16)) + b_ref[0]


def _ada_mod(c, ada_w, ada_b, tn=1024):
    depth, d, n = ada_w.shape
    bsz = c.shape[0]
    return pl.pallas_call(
        _ada_kernel,
        grid=(depth, n // tn),
        in_specs=[
            pl.BlockSpec((bsz, d), lambda l, j: (0, 0)),
            pl.BlockSpec((1, d, tn), lambda l, j: (l, 0, j)),
            pl.BlockSpec((1, 1, tn), lambda l, j: (l, 0, j)),
        ],
        out_specs=pl.BlockSpec((1, bsz, tn), lambda l, j: (l, 0, j)),
        out_shape=jax.ShapeDtypeStruct((depth, bsz, n), F32),
        name="ada_mod",
    )(c, ada_w, ada_b.reshape(depth, 1, n))


def _ffn_kernel(x_ref, g_ref, sh_ref, sc_ref, gt_ref, wup_ref, wdn_ref, o_ref, *, tf):
    x = x_ref[0]
    hb = _norm_mod(x, g_ref[...], sh_ref[0], sc_ref[0]).astype(BF16)
    acc = jnp.zeros(x.shape, F32)
    for f in range(D_FF // tf):
        gate = _dot(hb, wup_ref[:, f * tf:(f + 1) * tf])
        up = _dot(hb, wup_ref[:, D_FF + f * tf:D_FF + (f + 1) * tf])
        a = (_silu(gate) * up).astype(BF16)
        acc = acc + _dot(a, wdn_ref[f * tf:(f + 1) * tf, :])
    o_ref[0] = x + (0.5 * gt_ref[0]) * acc


def _ffn(x, mod_l, k0, gain, wup, wdn, layer, tm=512, tf=256):
    bsz, t, d = x.shape
    return pl.pallas_call(
        functools.partial(_ffn_kernel, tf=tf),
        grid=(bsz, t // tm),
        in_specs=[
            pl.BlockSpec((1, tm, d), lambda b, i: (b, i, 0)),
            _const_spec((1, d), single_buffer=False),
            _mod_spec(k0), _mod_spec(k0 + 1), _mod_spec(k0 + 2),
            _layer_spec((d, 2 * D_FF), layer),
            _layer_spec((D_FF, d), layer),
        ],
        out_specs=pl.BlockSpec((1, tm, d), lambda b, i: (b, i, 0)),
        out_shape=jax.ShapeDtypeStruct(x.shape, F32),
        compiler_params=pltpu.CompilerParams(
            dimension_semantics=("arbitrary", "arbitrary"), vmem_limit_bytes=VMEM_LIMIT),
        name="ffn",
    )(x, gain.reshape(1, d), mod_l, mod_l, mod_l, wup, wdn)


def _pack_w_in_kernel(w_ref, o_ref):
    gw = ATT_GROUP_W
    for g in range(N_GROUPS):
        for part in range(3):
            src = part * ATT_W + g * gw
            dst = P_ATT + (g * 3 + part) * gw
            o_ref[0, :, dst:dst + gw] = w_ref[0, :, src:src + gw].astype(BF16)
    n_dn = OFF_DN_A - OFF_DN_QKV
    o_ref[0, :, P_DN:P_DN + n_dn] = w_ref[0, :, OFF_DN_QKV:OFF_DN_A].astype(BF16)
    o_ref[0, :, P_MERGE:P_MERGE + 2 * D_MODEL] = w_ref[0, :, OFF_MERGE:N_IN].astype(BF16)
    ab = w_ref[0, :, OFF_DN_A:OFF_MERGE]
    pad = jnp.zeros((ab.shape[0], LANE - 2 * DN_HEADS), F32)
    o_ref[0, :, P_AB:P_AB + LANE] = jnp.concatenate([ab, pad], axis=1).astype(BF16)


def _pack_w_in(w, tk=256):
    depth, d, n = w.shape
    return pl.pallas_call(
        _pack_w_in_kernel,
        grid=(depth, d // tk),
        in_specs=[pl.BlockSpec((1, tk, n), lambda l, i: (l, i, 0))],
        out_specs=pl.BlockSpec((1, tk, P_TOTAL), lambda l, i: (l, i, 0)),
        out_shape=jax.ShapeDtypeStruct((depth, d, P_TOTAL), BF16),
        compiler_params=pltpu.CompilerParams(
            dimension_semantics=("arbitrary", "arbitrary"), vmem_limit_bytes=VMEM_LIMIT),
        name="pack_w_in",
    )(w)


def _inproj_kernel(x_ref, g_ref, sh_ref, sc_ref, w_ref, ones_ref, qn_ref, kn_ref, cw_ref,
                   za0_ref, za1_ref, za2_ref, dq_ref, dk_ref, dv_ref, gate_ref, mg_ref,
                   ab_ref, abt_ref, carry_ref, dscr_ref, *, tm, cn):
    x = x_ref[0]
    hb = _norm_mod(x, g_ref[...], sh_ref[0], sc_ref[0]).astype(BF16)

    @pl.when(pl.program_id(1) == 0)
    def _():
        carry_ref[...] = jnp.zeros(carry_ref.shape, F32)

    jobs = []
    gw = ATT_GROUP_W

    def att_store(za_ref, g, part, val):
        dil = DILATIONS[g]
        if dil == 1:
            za_ref[0, :, part * gw:(part + 1) * gw] = val.astype(BF16)
            return
        for s in range(gw // LANE):
            dscr_ref[s] = val[:, s * LANE:(s + 1) * LANE]
        for r in range(dil):
            for s in range(gw // LANE):
                lo = r * 3 * gw + part * gw + s * LANE
                za_ref[0, :, lo:lo + LANE] = dscr_ref[s, pl.ds(r, tm // dil, stride=dil), :].astype(BF16)

    def att_norm(za_ref, g, part, gain_ref, mult):
        def epilogue(z):
            ms = _dot((z * z).astype(BF16), ones_ref[...])
            att_store(za_ref, g, part, (z * lax.rsqrt(ms + EPS)) * (gain_ref[...] * mult))
        return epilogue

    def att_plain(za_ref, g):
        def epilogue(z):
            att_store(za_ref, g, 2, z)
        return epilogue

    light = []
    for g, za_ref in enumerate((za0_ref, za1_ref, za2_ref)):
        c0 = P_ATT + g * 3 * gw
        light.append((c0, gw, att_norm(za_ref, g, 0, qn_ref, ATT_HEAD_DIM ** -0.5)))
        light.append((c0 + gw, gw, att_norm(za_ref, g, 1, kn_ref, 1.0)))
        light.append((c0 + 2 * gw, gw, att_plain(za_ref, g)))

    def dn_conv(out_ref, part, j):
        dc = part * DN_W + j * cn

        def epilogue(z):
            xe = jnp.concatenate([carry_ref[:, dc:dc + cn], z], axis=0)
            carry_ref[:, dc:dc + cn] = z[tm - SUBLANE:tm, :]
            y = cw_ref[CONV_W - 1:CONV_W, dc:dc + cn] * z
            for s in range(1, CONV_W):
                shifted = pltpu.roll(xe, s, axis=0)[SUBLANE:, :]
                y = y + cw_ref[CONV_W - 1 - s:CONV_W - s, dc:dc + cn] * shifted
            y = _silu(y)
            if part < 2:
                mult = DN_HEAD_DIM ** -0.5 if part == 0 else 1.0
                for h in range(cn // DN_HEAD_DIM):
                    yh = y[:, h * DN_HEAD_DIM:(h + 1) * DN_HEAD_DIM]
                    ss = jnp.sum(yh * yh, axis=-1, keepdims=True)
                    yn = yh * (lax.rsqrt(ss + EPS) * mult)
                    lo = j * cn + h * DN_HEAD_DIM
                    out_ref[0, :, lo:lo + DN_HEAD_DIM] = yn.astype(BF16)
            else:
                out_ref[0, :, j * cn:(j + 1) * cn] = y.astype(BF16)
        return epilogue

    heavy = []
    for part, out_ref in enumerate((dq_ref, dk_ref, dv_ref)):
        for j in range(DN_W // cn):
            heavy.append((P_DN + part * DN_W + j * cn, cn, dn_conv(out_ref, part, j)))

    def store_cast(out_ref, j, fn):
        def epilogue(z):
            out_ref[0, :, j * cn:(j + 1) * cn] = fn(z).astype(BF16)
        return epilogue

    for j in range(DN_W // cn):
        light.append((P_GATE + j * cn, cn, store_cast(gate_ref, j, lambda z: z)))
    for j in range(2 * D_MODEL // cn):
        light.append((P_MERGE + j * cn, cn, store_cast(mg_ref, j, _sigmoid)))

    def ab_epilogue(z):
        ab_ref[0] = z[:, :2 * DN_HEADS]
        abt_ref[0] = z.T[:2 * DN_HEADS, :]

    light.append((P_AB, 128, ab_epilogue))

    per_heavy = -(-len(light) // len(heavy))
    for i, job in enumerate(heavy):
        jobs.append(job)
        jobs.extend(light[i * per_heavy:(i + 1) * per_heavy])

    def project(job):
        return _dot(hb, w_ref[:, job[0]:job[0] + job[1]])

    pending = [project(job) for job in jobs[:INPROJ_LOOKAHEAD]]
    for i, job in enumerate(jobs):
        if i + INPROJ_LOOKAHEAD < len(jobs):
            pending.append(project(jobs[i + INPROJ_LOOKAHEAD]))
        job[2](pending.pop(0))


def _inproj(x, mod_l, gain, w_packed, layer, ones_blk, qn4, kn4, conv_w, tm=512, cn=512):
    bsz, t, d = x.shape
    bf = lambda n: jax.ShapeDtypeStruct((bsz, t, n), BF16)
    row = lambda n: pl.BlockSpec((1, tm, n), lambda b, i: (b, i, 0))
    att_spec = lambda dil: pl.BlockSpec((1, tm // dil, dil * 3 * ATT_GROUP_W), lambda b, i: (b, i, 0))
    att_shape = lambda dil: jax.ShapeDtypeStruct((bsz, t // dil, dil * 3 * ATT_GROUP_W), BF16)
    return pl.pallas_call(
        functools.partial(_inproj_kernel, tm=tm, cn=cn),
        grid=(bsz, t // tm),
        in_specs=[
            row(d),
            _const_spec((1, d), single_buffer=False),
            _mod_spec(3), _mod_spec(4),
            _layer_spec((d, P_TOTAL), layer),
            _const_spec((ATT_GROUP_W, ATT_GROUP_W)),
            _const_spec((1, ATT_GROUP_W), single_buffer=False),
            _const_spec((1, ATT_GROUP_W), single_buffer=False),
            _const_spec((CONV_W, 3 * DN_W), single_buffer=False),
        ],
        out_specs=[att_spec(dil) for dil in DILATIONS] + [row(DN_W)] * 4 + [row(2 * D_MODEL)]
        + [row(2 * DN_HEADS), pl.BlockSpec((1, 2 * DN_HEADS, tm), lambda b, i: (b, 0, i))],
        out_shape=[att_shape(dil) for dil in DILATIONS] + [bf(DN_W)] * 4 + [bf(2 * D_MODEL)]
        + [jax.ShapeDtypeStruct((bsz, t, 2 * DN_HEADS), F32),
           jax.ShapeDtypeStruct((bsz, 2 * DN_HEADS, t), F32)],
        scratch_shapes=[pltpu.VMEM((SUBLANE, 3 * DN_W), F32),
                        pltpu.VMEM((ATT_GROUP_W // LANE, tm, LANE), F32)],
        compiler_params=pltpu.CompilerParams(
            dimension_semantics=("arbitrary", "arbitrary"), vmem_limit_bytes=VMEM_LIMIT),
        name="inproj",
    )(x, gain.reshape(1, d), mod_l, mod_l, w_packed, ones_blk, qn4, kn4, conv_w)


def _attn_scores(q, kk, bias, head_masks):
    zero = jnp.zeros_like(q)
    qs = jnp.concatenate([jnp.where(hm, q, zero) for hm in head_masks], axis=0)
    return _dot_nt(qs, kk) + jnp.concatenate([bias] * ATT_HEADS, axis=0)


def _attn_probs(s):
    m = jnp.max(s, axis=-1, keepdims=True)
    p = jnp.exp(s - m)
    l = jnp.sum(p, axis=-1, keepdims=True)
    return p.astype(BF16), 1.0 / l, m + jnp.log(l)


def _attn_out(pv, inv, lse, head_masks):
    nq = pv.shape[0] // ATT_HEADS
    o = jnp.zeros((nq, ATT_GROUP_W), F32)
    ls = jnp.zeros((nq, ATT_GROUP_W), F32)
    for h, hm in enumerate(head_masks):
        rows = slice(h * nq, (h + 1) * nq)
        o = jnp.where(hm, pv[rows] * inv[rows], o)
        ls = jnp.where(hm, jnp.broadcast_to(lse[rows], (nq, ATT_GROUP_W)), ls)
    return o, ls


def _attn_kernel(q1_ref, k1_ref, v1_ref, k1p_ref, v1p_ref, q2_ref, k2_ref, v2_ref, z3_ref,
                 o1_ref, l1_ref, o2_ref, l2_ref, o3_ref, l3_ref):
    nb = ATT_BLOCK
    gw = ATT_GROUP_W
    step = pl.program_id(1)
    lane = lax.broadcasted_iota(jnp.int32, (1, gw), 1)
    head_masks = [(lane // ATT_HEAD_DIM) == h for h in range(ATT_HEADS)]
    qi = lax.broadcasted_iota(jnp.int32, (nb, 2 * nb), 0)
    kj = lax.broadcasted_iota(jnp.int32, (nb, 2 * nb), 1)
    ok_two = (kj >= qi) & (kj <= qi + nb)
    ok_cur = (kj >= nb) & (kj <= qi + nb)
    bias_two = jnp.where(ok_two, 0.0, NEG_BIG).astype(F32)
    bias_first = jnp.where(ok_cur, 0.0, NEG_BIG).astype(F32)
    bias_one = bias_first[:, nb:]

    def blocks(i):
        return slice(i * nb, (i + 1) * nb), slice((i - 1) * nb, (i + 1) * nb)

    def group0():
        jobs = []
        for i in range(4):
            rows, keys = blocks(i)
            if i == 0:
                kk = jnp.concatenate([k1p_ref[0], k1_ref[0, rows]], axis=0)
                vv = jnp.concatenate([v1p_ref[0], v1_ref[0, rows]], axis=0)
                bias = jnp.where(step == 0, bias_first, bias_two)
            else:
                kk, vv, bias = k1_ref[0, keys], v1_ref[0, keys], bias_two
            jobs.append((q1_ref[0, rows], kk, vv, bias, (o1_ref, l1_ref, rows, slice(None))))
        return jobs

    def group1():
        jobs = []
        for i in range(4):
            rows, keys = blocks(i)
            if i == 0:
                kk, vv, bias = k2_ref[0, rows], v2_ref[0, rows], bias_one
            else:
                kk, vv, bias = k2_ref[0, keys], v2_ref[0, keys], bias_two
            jobs.append((q2_ref[0, rows], kk, vv, bias, (o2_ref, l2_ref, rows, slice(None))))
        return jobs

    def group2():
        jobs = []
        for r in range(4):
            c0 = r * 3 * gw
            jobs.append((z3_ref[0, :, c0:c0 + gw], z3_ref[0, :, c0 + gw:c0 + 2 * gw],
                         z3_ref[0, :, c0 + 2 * gw:c0 + 3 * gw], bias_one,
                         (o3_ref, l3_ref, slice(None), slice(r * gw, (r + 1) * gw))))
        return jobs

    def scores(jobs):
        return [_attn_scores(q, kk, bias, head_masks) for q, kk, _, bias, _ in jobs]

    def finish(jobs, ss):
        probs = [_attn_probs(s) for s in ss]
        pvs = [_dot(p, job[2]) for (p, _, _), job in zip(probs, jobs)]
        for pv, (_, inv, lse), job in zip(pvs, probs, jobs):
            o_ref, l_ref, rows, cols = job[4]
            o, ls = _attn_out(pv, inv, lse, head_masks)
            o_ref[0, rows, cols] = o
            l_ref[0, rows, cols] = ls

    j0, j1, j2 = group0(), group1(), group2()
    s0 = scores(j0)
    s1 = scores(j1)
    finish(j0, s0)
    s2 = scores(j2)
    finish(j1, s1)
    finish(j2, s2)


def _attention(za0, z2, z3):
    bsz, t, _ = za0.shape
    gw, nb = ATT_GROUP_W, ATT_BLOCK
    steps = 4
    rows1 = t // steps
    assert rows1 == 4 * nb and t // DILATIONS[1] == 4 * nb and t // DILATIONS[2] == nb
    prev = lambda b, j: (b, jnp.maximum(4 * j - 1, 0), 1)
    prev_v = lambda b, j: (b, jnp.maximum(4 * j - 1, 0), 2)
    in_specs = [
        pl.BlockSpec((1, rows1, gw), lambda b, j: (b, j, 0)),
        pl.BlockSpec((1, rows1, gw), lambda b, j: (b, j, 1)),
        pl.BlockSpec((1, rows1, gw), lambda b, j: (b, j, 2)),
        pl.BlockSpec((1, nb, gw), prev),
        pl.BlockSpec((1, nb, gw), prev_v),
        pl.BlockSpec((1, 4 * nb, gw), lambda b, j: (b, 0, 3 * j)),
        pl.BlockSpec((1, 4 * nb, gw), lambda b, j: (b, 0, 3 * j + 1)),
        pl.BlockSpec((1, 4 * nb, gw), lambda b, j: (b, 0, 3 * j + 2)),
        pl.BlockSpec((1, nb, 4 * 3 * gw), lambda b, j: (b, 0, j)),
    ]
    o1s = pl.BlockSpec((1, rows1, gw), lambda b, j: (b, j, 0))
    o2s = pl.BlockSpec((1, 4 * nb, gw), lambda b, j: (b, 0, j))
    o3s = pl.BlockSpec((1, nb, 4 * gw), lambda b, j: (b, 0, j))
    s1 = jax.ShapeDtypeStruct((bsz, t, gw), F32)
    s2 = jax.ShapeDtypeStruct((bsz, t // DILATIONS[1], DILATIONS[1] * gw), F32)
    s3 = jax.ShapeDtypeStruct((bsz, t // DILATIONS[2], DILATIONS[2] * gw), F32)
    o1, l1, o2, l2, o3, l3 = pl.pallas_call(
        _attn_kernel,
        grid=(bsz, steps),
        in_specs=in_specs,
        out_specs=[o1s, o1s, o2s, o2s, o3s, o3s],
        out_shape=[s1, s1, s2, s2, s3, s3],
        compiler_params=pltpu.CompilerParams(
            dimension_semantics=("arbitrary", "arbitrary"), vmem_limit_bytes=VMEM_LIMIT),
        name="dilated_attn",
    )(za0, za0, za0, za0, za0, z2, z2, z2, z3)
    return o1, l1, o2, l2, o3, l3


def _softplus(x):
    return jnp.maximum(x, 0.0) + jnp.log(1.0 + jnp.exp(-jnp.abs(x)))


def _chunk_masks():
    c = DN_CHUNK
    ii = lax.broadcasted_iota(jnp.int32, (c, c), 0)
    jj = lax.broadcasted_iota(jnp.int32, (c, c), 1)
    return ii >= jj, ii > jj, ii <= jj


def _chunk_decay(ab, abt, alog_ref, dtb_ref, alog_c_ref, dtb_c_ref, tril, triu):
    g_col = -jnp.exp(alog_ref[...]) * _softplus(ab + dtb_ref[...])
    g_row = -jnp.exp(alog_c_ref[...]) * _softplus(abt + dtb_c_ref[...])
    gc_col = _dot_exact(jnp.where(tril, 1.0, 0.0).astype(F32), g_col)
    gc_row = _dot_exact(g_row, jnp.where(triu, 1.0, 0.0).astype(F32))
    return gc_col, gc_row, _sigmoid(ab)


def _head_decay(gc_col, gc_row, beta_col, h, tril):
    c, hd = DN_CHUNK, DN_HEAD_DIM
    cb = jnp.broadcast_to(gc_col[:, h:h + 1], (c, hd))
    bb = jnp.broadcast_to(beta_col[:, DN_HEADS + h:DN_HEADS + h + 1], (c, hd))
    rb = jnp.broadcast_to(gc_row[h:h + 1, :], (c, c))
    ld = jnp.exp(jnp.where(tril, cb[:, :c] - rb, 0.0))
    return cb, bb, ld


def _dn_prep_kernel(k_ref, ab_ref, abt_ref, alog_ref, dtb_ref, alog_c_ref, dtb_c_ref, a_ref,
                    gcc_ref, gcr_ref, *, n_chunks):
    c, hd = DN_CHUNK, DN_HEAD_DIM
    tril, strict, triu = _chunk_masks()
    lane16 = lax.broadcasted_iota(jnp.int32, (c, 2 * DN_HEADS), 1)
    for n in range(n_chunks):
        r0 = n * c
        gc_col, gc_row, beta_col = _chunk_decay(ab_ref[0, r0:r0 + c, :], abt_ref[0, n], alog_ref,
                                                dtb_ref, alog_c_ref, dtb_c_ref, tril, triu)
        gcc_ref[0, r0:r0 + c, :] = jnp.where(lane16 < DN_HEADS, gc_col, beta_col)
        gcr_ref[0, n] = gc_row
        for h in range(DN_HEADS):
            k = k_ref[0, r0:r0 + c, h * hd:(h + 1) * hd]
            _, bb, ld = _head_decay(gc_col, gc_row, beta_col, h, tril)
            a_ref[0, n, h] = jnp.where(strict, _dot_nt(k, k) * (bb[:, :c] * ld), 0.0)


def _dn_solve_kernel(a_ref, t_ref, at_scr, tt_scr):
    c = DN_CHUNK
    tiles = c * c // LANE
    per = LANE // c
    for t in range(tiles):
        blk = a_ref[:, t * LANE:(t + 1) * LANE].T
        for r in range(per):
            at_scr[t * per + r] = blk[r * c:(r + 1) * c]
    row_id = lax.broadcasted_iota(jnp.int32, (SUBLANE, LANE), 0)
    zero = jnp.zeros((SUBLANE, LANE), F32)
    for i in range(c):
        groups = i // SUBLANE + 1
        acc = [zero] * (c // SUBLANE)
        acc[groups - 1] = jnp.where(row_id == i % SUBLANE, 1.0, 0.0).astype(F32)
        for j in range(i):
            a = jnp.broadcast_to(at_scr[i, j:j + 1, :], (SUBLANE, LANE))
            for g in range(j // SUBLANE + 1):
                acc[g] = acc[g] - a * tt_scr[j, g * SUBLANE:(g + 1) * SUBLANE, :]
        for g in range(c // SUBLANE):
            tt_scr[i, g * SUBLANE:(g + 1) * SUBLANE, :] = acc[g]
    for t in range(tiles):
        blk = jnp.concatenate([tt_scr[t * per + r] for r in range(per)], axis=0)
        t_ref[:, t * LANE:(t + 1) * LANE] = blk.T.astype(t_ref.dtype)


def _dn_scan_kernel(q_ref, k_ref, v_ref, gcc_ref, gcr_ref, t_ref, o_ref, s_ref, *, n_chunks):
    c, hd = DN_CHUNK, DN_HEAD_DIM

    @pl.when(pl.program_id(1) == 0)
    def _():
        s_ref[...] = jnp.zeros(s_ref.shape, F32)

    tril, _, _ = _chunk_masks()
    heads = range(DN_HEADS)

    def stage_a(n):
        r0 = n * c
        gc_col = beta_col = gcc_ref[0, r0:r0 + c, :]
        gc_row = gcr_ref[0, n]
        dec, qk, xs, egc = [], [], [], []
        for h in heads:
            dec.append(_head_decay(gc_col, gc_row, beta_col, h, tril))
        for h in heads:
            lanes = slice(h * hd, (h + 1) * hd)
            qk.append(_dot_nt(q_ref[0, r0:r0 + c, lanes], k_ref[0, r0:r0 + c, lanes]))
        for h in heads:
            lanes = slice(h * hd, (h + 1) * hd)
            cb, bb, _ = dec[h]
            eb = jnp.exp(cb)
            egc.append(eb.astype(BF16))
            rhs = jnp.concatenate([v_ref[0, r0:r0 + c, lanes] * bb.astype(BF16),
                                   k_ref[0, r0:r0 + c, lanes] * (bb * eb).astype(BF16)], axis=1)
            xs.append(_dot(t_ref[0, n, h], rhs))
        per_head = []
        for h in heads:
            lanes = slice(h * hd, (h + 1) * hd)
            cb, _, ld = dec[h]
            k = k_ref[0, r0:r0 + c, lanes].astype(F32)
            gl = cb[c - 1:c, :]
            attn = jnp.where(tril, qk[h] * ld, 0.0).astype(BF16)
            kdec_t = (k * jnp.exp(gl - cb)).T.astype(BF16)
            lhs_s = jnp.concatenate([xs[h][:, hd:].astype(BF16),
                                     q_ref[0, r0:r0 + c, lanes] * egc[h]], axis=0)
            per_head.append((xs[h][:, :hd], lhs_s, jnp.concatenate([attn, kdec_t], axis=0),
                             jnp.exp(gl)))
        return per_head

    state = [s_ref[h] for h in heads]
    cur = stage_a(0)
    for n in range(n_chunks):
        r0 = n * c
        ws = [_dot(cur[h][1], state[h].astype(BF16)) for h in heads]
        nxt = stage_a(n + 1) if n + 1 < n_chunks else None
        vn = [(cur[h][0] - ws[h][:c]).astype(BF16) for h in heads]
        upd = [_dot(cur[h][2], vn[h]) for h in heads]
        for h in heads:
            lanes = slice(h * hd, (h + 1) * hd)
            o_ref[0, r0:r0 + c, lanes] = (ws[h][c:] + upd[h][:c]).astype(BF16)
            state[h] = state[h] * cur[h][3] + upd[h][c:]
        cur = nxt
    for h in heads:
        s_ref[h] = state[h]


def _deltanet(dq, dk, dv, ab, abt, a_log, dt_bias, rows=256):
    bsz, t, w = dq.shape
    c = DN_CHUNK
    n_chunks = rows // c
    total_chunks = t // c
    abt_c = abt.reshape(bsz, 2 * DN_HEADS, total_chunks, c).transpose(0, 2, 1, 3)
    pad = jnp.zeros((DN_HEADS,), F32)
    alog16 = jnp.concatenate([a_log, pad])
    dtb16 = jnp.concatenate([dt_bias, pad])
    small = [alog16.reshape(1, -1), dtb16.reshape(1, -1), alog16.reshape(-1, 1), dtb16.reshape(-1, 1)]
    small_specs = [_const_spec((1, 2 * DN_HEADS), single_buffer=False)] * 2 + \
                  [_const_spec((2 * DN_HEADS, 1), single_buffer=False)] * 2
    row = pl.BlockSpec((1, rows, w), lambda b, i: (b, i, 0))
    ab_spec = pl.BlockSpec((1, rows, 2 * DN_HEADS), lambda b, i: (b, i, 0))
    abt_spec = pl.BlockSpec((1, n_chunks, 2 * DN_HEADS, c), lambda b, i: (b, i, 0, 0))
    mat_spec = pl.BlockSpec((1, n_chunks, DN_HEADS, c, c), lambda b, i: (b, i, 0, 0, 0))
    mat_shape = jax.ShapeDtypeStruct((bsz, total_chunks, DN_HEADS, c, c), F32)
    params = pltpu.CompilerParams(dimension_semantics=("arbitrary", "arbitrary"),
                                  vmem_limit_bytes=VMEM_LIMIT)

    a_mat, gcc, gcr = pl.pallas_call(
        functools.partial(_dn_prep_kernel, n_chunks=n_chunks),
        grid=(bsz, t // rows),
        in_specs=[row, ab_spec, abt_spec] + small_specs,
        out_specs=[mat_spec, ab_spec, abt_spec],
        out_shape=[mat_shape, jax.ShapeDtypeStruct(ab.shape, F32),
                   jax.ShapeDtypeStruct(abt_c.shape, F32)],
        compiler_params=params,
        name="dn_prep",
    )(dk, ab, abt_c, *small)

    n_sys = bsz * total_chunks * DN_HEADS
    assert n_sys % LANE == 0
    flat_spec = pl.BlockSpec((LANE, c * c), lambda i: (i, 0))
    t_mat = pl.pallas_call(
        _dn_solve_kernel,
        grid=(n_sys // LANE,),
        in_specs=[flat_spec],
        out_specs=flat_spec,
        out_shape=jax.ShapeDtypeStruct((n_sys, c * c), BF16),
        scratch_shapes=[pltpu.VMEM((c, c, LANE), F32), pltpu.VMEM((c, c, LANE), F32)],
        compiler_params=pltpu.CompilerParams(dimension_semantics=("arbitrary",),
                                             vmem_limit_bytes=VMEM_LIMIT),
        name="dn_solve",
    )(a_mat.reshape(n_sys, c * c)).reshape(mat_shape.shape)

    return pl.pallas_call(
        functools.partial(_dn_scan_kernel, n_chunks=n_chunks),
        grid=(bsz, t // rows),
        in_specs=[row, row, row, ab_spec, abt_spec, mat_spec],
        out_specs=row,
        out_shape=jax.ShapeDtypeStruct((bsz, t, w), BF16),
        scratch_shapes=[pltpu.VMEM((DN_HEADS, DN_HEAD_DIM, DN_HEAD_DIM), F32)],
        compiler_params=params,
        name="dn_scan",
    )(dq, dk, dv, gcc, gcr, t_mat)


def _merge_kernel(o1_ref, l1_ref, o2_ref, l2_ref, o3_ref, l3_ref, odn_ref, gate_ref, nrm_ref,
                  mg_ref, x_ref, gt_ref, wpa_ref, wpd_ref, wo_ref, out_ref, nat_scr):
    d = D_MODEL
    tm = x_ref.shape[1]
    slabs = ATT_GROUP_W // LANE
    for a, (ref, dil) in enumerate(((o2_ref, DILATIONS[1]), (l2_ref, DILATIONS[1]),
                                    (o3_ref, DILATIONS[2]), (l3_ref, DILATIONS[2]))):
        for r in range(dil):
            for s in range(slabs):
                lo = r * ATT_GROUP_W + s * LANE
                nat_scr[a, s, pl.ds(r, tm // dil, stride=dil), :] = ref[0, :, lo:lo + LANE]
    halves = [slice(i * (tm // 2), (i + 1) * (tm // 2)) for i in range(2)]
    yd = []
    for r in halves:
        gated = []
        for h in range(DN_HEADS):
            lanes = slice(h * DN_HEAD_DIM, (h + 1) * DN_HEAD_DIM)
            o = odn_ref[0, r, lanes].astype(F32)
            ms = jnp.mean(o * o, axis=-1, keepdims=True)
            on = (o * lax.rsqrt(ms + EPS)) * nrm_ref[...]
            gated.append((on * _silu(gate_ref[0, r, lanes].astype(F32))).astype(BF16))
        yd.append(_dot(jnp.concatenate(gated, axis=1), wpd_ref[...]))
    ya = []
    for r in halves:
        ys = []
        for s in range(slabs):
            cols = slice(s * LANE, (s + 1) * LANE)
            l1, l2, l3 = l1_ref[0, r, cols], nat_scr[1, s, r, :], nat_scr[3, s, r, :]
            m = jnp.maximum(jnp.maximum(l1, l2), l3)
            e1, e2, e3 = jnp.exp(l1 - m), jnp.exp(l2 - m), jnp.exp(l3 - m)
            y = e1 * o1_ref[0, r, cols] + e2 * nat_scr[0, s, r, :] + e3 * nat_scr[2, s, r, :]
            ys.append((y / (e1 + e2 + e3)).astype(BF16))
        ya.append(_dot(jnp.concatenate(ys, axis=1), wpa_ref[...]))
    for i, r in enumerate(halves):
        mix = mg_ref[0, r, :d].astype(F32) * ya[i] + mg_ref[0, r, d:].astype(F32) * yd[i]
        out = _dot(mix.astype(BF16), wo_ref[...])
        out_ref[0, r] = x_ref[0, r] + gt_ref[0] * out


def _merge(att, odn, gate, dn_norm, mg, x, mod_l, wpa, wpd, wo, layer, tm=512):
    bsz, t, d = x.shape
    row = lambda n: pl.BlockSpec((1, tm, n), lambda b, i: (b, i, 0))
    view = lambda dil: pl.BlockSpec((1, tm // dil, dil * ATT_GROUP_W), lambda b, i: (b, i, 0))
    att_specs = [row(ATT_GROUP_W)] * 2 + [view(DILATIONS[1])] * 2 + [view(DILATIONS[2])] * 2
    return pl.pallas_call(
        _merge_kernel,
        grid=(bsz, t // tm),
        in_specs=att_specs + [row(DN_W), row(DN_W), _const_spec((1, DN_HEAD_DIM), single_buffer=False),
                              row(2 * d), row(d), _mod_spec(5),
                              _layer_spec(wpa.shape[1:], layer), _layer_spec(wpd.shape[1:], layer),
                              _layer_spec(wo.shape[1:], layer)],
        out_specs=row(d),
        out_shape=jax.ShapeDtypeStruct(x.shape, F32),
        scratch_shapes=[pltpu.VMEM((4, ATT_GROUP_W // LANE, tm, LANE), F32)],
        compiler_params=pltpu.CompilerParams(
            dimension_semantics=("arbitrary", "arbitrary"), vmem_limit_bytes=VMEM_LIMIT),
        name="merge_out",
    )(*att, odn, gate, dn_norm.reshape(1, -1), mg, x, mod_l, wpa, wpd, wo)


def _mixer(x, mod_l, layer, gain, w_packed, ones_blk, qn4, kn4, conv_w, a_log, dt_bias, dn_norm,
           wpa, wpd, wo):
    za0, za1, za2, dq, dk, dv, gate, mg, ab, abt = _inproj(
        x, mod_l, gain, w_packed, layer, ones_blk, qn4, kn4, conv_w)
    att = _attention(za0, za1, za2)
    odn = _deltanet(dq, dk, dv, ab, abt, a_log, dt_bias)
    return _merge(att, odn, gate, dn_norm, mg, x, mod_l, wpa, wpd, wo, layer)


def kernel(x, c, ada_w, ada_b, norm_ff1, ffn1_w_up, ffn1_w_down, norm_mix, w_in, q_norm, k_norm,
           conv_w, a_log, dt_bias, dn_norm, w_proj_att, w_proj_dn, w_out, norm_ff2, ffn2_w_up,
           ffn2_w_down):
    bsz = x.shape[0]
    mod = _ada_mod(c, ada_w, ada_b)
    blk = jnp.arange(ATT_GROUP_W) // ATT_HEAD_DIM
    ones_blk = jnp.where(blk[:, None] == blk[None, :], 1.0 / ATT_HEAD_DIM, 0.0).astype(BF16)
    w_in_packed = _pack_w_in(w_in)
    up1, dn1, up2, dn2 = (w.astype(BF16) for w in (ffn1_w_up, ffn1_w_down, ffn2_w_up, ffn2_w_down))
    wpa, wpd, wo = (w.astype(BF16) for w in (w_proj_att, w_proj_dn, w_out))
    for l in range(DEPTH):
        mod_l = mod[l].reshape(bsz, 1, N_ADA * D_MODEL)
        x = _ffn(x, mod_l, 0, norm_ff1[l], up1, dn1, l)
        x = _mixer(x, mod_l, l, norm_mix[l], w_in_packed, ones_blk,
                   jnp.tile(q_norm[l], ATT_HEADS).reshape(1, -1),
                   jnp.tile(k_norm[l], ATT_HEADS).reshape(1, -1),
                   conv_w[l], a_log[l], dt_bias[l], dn_norm[l], wpa, wpd, wo)
        x = _ffn(x, mod_l, 6, norm_ff2[l], up2, dn2, l)
    return x
```

```python
import functools

import jax
import jax.numpy as jnp
from jax import lax
from jax.experimental import pallas as pl
from jax.experimental.pallas import tpu as pltpu

F32 = jnp.float32
BF16 = jnp.bfloat16

D_MODEL = 1024
DEPTH = 4
SEQ = 2048
D_FF = 2816
EPS = 1e-6
N_ADA = 9

ATT_HEAD_DIM = 64
ATT_HEADS = 4
ATT_GROUP_W = ATT_HEADS * ATT_HEAD_DIM
DILATIONS = (1, 4, 16)
ATT_BLOCK = 128
N_GROUPS = 3
ATT_W = N_GROUPS * ATT_GROUP_W

DN_HEADS = 8
DN_HEAD_DIM = 128
DN_W = DN_HEADS * DN_HEAD_DIM
DN_CHUNK = 64
CONV_W = 4

OFF_DN_QKV = 3 * ATT_W
OFF_DN_GATE = OFF_DN_QKV + 3 * DN_W
OFF_DN_A = OFF_DN_GATE + DN_W
OFF_DN_B = OFF_DN_A + DN_HEADS
OFF_MERGE = OFF_DN_B + DN_HEADS
N_IN = OFF_MERGE + 2 * D_MODEL

P_ATT = 0
P_DN = P_ATT + 3 * ATT_W
P_GATE = P_DN + 3 * DN_W
P_MERGE = P_GATE + DN_W
P_AB = P_MERGE + 2 * D_MODEL
P_TOTAL = P_AB + 256

LANE = 128
SUBLANE = 8
NEG_BIG = -1e30
VMEM_LIMIT = 56 * 1024 * 1024
INPROJ_LOOKAHEAD = 1


def _dot(a, b):
    return jnp.dot(a, b, preferred_element_type=F32)


def _dot_nt(a, b):
    return lax.dot_general(a, b, (((1,), (1,)), ((), ())), preferred_element_type=F32)


def _dot_exact(a, b):
    return jnp.dot(a, b, preferred_element_type=F32, precision=lax.Precision.HIGHEST)


def _sigmoid(x):
    return 1.0 / (1.0 + jnp.exp(-x))


def _silu(x):
    return x * _sigmoid(x)


def _norm_mod(x, gain, shift, scale):
    ms = jnp.mean(x * x, axis=-1, keepdims=True)
    y = (x * lax.rsqrt(ms + EPS)) * gain
    return y * (1.0 + scale) + shift


def _const_spec(shape, single_buffer=True):
    nd = len(shape)
    kwargs = {"pipeline_mode": pl.Buffered(1)} if single_buffer else {}
    return pl.BlockSpec(shape, lambda *_: (0,) * nd, **kwargs)


def _layer_spec(shape, layer):
    nd = len(shape)
    return pl.BlockSpec((None,) + tuple(shape), lambda *_: (layer,) + (0,) * nd,
                        pipeline_mode=pl.Buffered(1))


def _mod_spec(k):
    return pl.BlockSpec((1, 1, D_MODEL), lambda b, i: (b, 0, k))


def _ada_kernel(c_ref, w_ref, b_ref, o_ref):
    c = c_ref[...]
    ca = _silu(c).astype(BF16)
    o_ref[0] = _dot(ca, w_ref[0].astype(BF16)) + b_ref[0]


def _ada_mod(c, ada_w, ada_b, tn=1024):
    depth, d, n = ada_w.shape
    bsz = c.shape[0]
    return pl.pallas_call(
        _ada_kernel,
        grid=(depth, n // tn),
        in_specs=[
            pl.BlockSpec((bsz, d), lambda l, j: (0, 0)),
            pl.BlockSpec((1, d, tn), lambda l, j: (l, 0, j)),
            pl.BlockSpec((1, 1, tn), lambda l, j: (l, 0, j)),
        ],
        out_specs=pl.BlockSpec((1, bsz, tn), lambda l, j: (l, 0, j)),
        out_shape=jax.ShapeDtypeStruct((depth, bsz, n), F32),
        name="ada_mod",
    )(c, ada_w, ada_b.reshape(depth, 1, n))


def _ffn_kernel(x_ref, g_ref, sh_ref, sc_ref, gt_ref, wup_ref, wdn_ref, o_ref, *, tf):
    x = x_ref[0]
    hb = _norm_mod(x, g_ref[...], sh_ref[0], sc_ref[0]).astype(BF16)
    acc = jnp.zeros(x.shape, F32)
    for f in range(D_FF // tf):
        gate = _dot(hb, wup_ref[:, f * tf:(f + 1) * tf])
        up = _dot(hb, wup_ref[:, D_FF + f * tf:D_FF + (f + 1) * tf])
        a = (_silu(gate) * up).astype(BF16)
        acc = acc + _dot(a, wdn_ref[f * tf:(f + 1) * tf, :])
    o_ref[0] = x + (0.5 * gt_ref[0]) * acc


def _ffn(x, mod_l, k0, gain, wup, wdn, layer, tm=512, tf=256):
    bsz, t, d = x.shape
    return pl.pallas_call(
        functools.partial(_ffn_kernel, tf=tf),
        grid=(bsz, t // tm),
        in_specs=[
            pl.BlockSpec((1, tm, d), lambda b, i: (b, i, 0)),
            _const_spec((1, d), single_buffer=False),
            _mod_spec(k0), _mod_spec(k0 + 1), _mod_spec(k0 + 2),
            _layer_spec((d, 2 * D_FF), layer),
            _layer_spec((D_FF, d), layer),
        ],
        out_specs=pl.BlockSpec((1, tm, d), lambda b, i: (b, i, 0)),
        out_shape=jax.ShapeDtypeStruct(x.shape, F32),
        compiler_params=pltpu.CompilerParams(
            dimension_semantics=("arbitrary", "arbitrary"), vmem_limit_bytes=VMEM_LIMIT),
        name="ffn",
    )(x, gain.reshape(1, d), mod_l, mod_l, mod_l, wup, wdn)


PACK_TILE = 256


def _pack_starts():
    gw = ATT_GROUP_W
    starts = [part * ATT_W + g * gw for g in range(N_GROUPS) for part in range(3)]
    starts += list(range(OFF_DN_QKV, OFF_DN_A, PACK_TILE))
    starts += list(range(OFF_MERGE, N_IN, PACK_TILE))
    starts.append(OFF_DN_A)
    assert len(starts) * PACK_TILE == P_TOTAL
    return starts


def _pack_w_in_kernel(starts_ref, w_ref, o_ref):
    del starts_ref
    t = w_ref[0].T
    last = pl.program_id(1) == pl.num_programs(1) - 1
    keep = jnp.where(last, 2 * DN_HEADS, PACK_TILE)
    lane = lax.broadcasted_iota(jnp.int32, (1, PACK_TILE), 1)
    o_ref[0] = jnp.where(lane < keep, t, 0.0).astype(BF16)


def _pack_w_in(w):
    depth, d, n = w.shape
    starts = jnp.asarray(_pack_starts(), jnp.int32)
    grid_spec = pltpu.PrefetchScalarGridSpec(
        num_scalar_prefetch=1,
        grid=(depth, P_TOTAL // PACK_TILE),
        in_specs=[pl.BlockSpec((pl.Element(1), pl.Element(PACK_TILE), pl.Element(d)),
                               lambda l, j, st: (l, pl.multiple_of(st[j], SUBLANE), 0))],
        out_specs=pl.BlockSpec((1, d, PACK_TILE), lambda l, j, st: (l, 0, j)),
    )
    return pl.pallas_call(
        _pack_w_in_kernel,
        grid_spec=grid_spec,
        out_shape=jax.ShapeDtypeStruct((depth, d, P_TOTAL), BF16),
        compiler_params=pltpu.CompilerParams(
            dimension_semantics=("arbitrary", "arbitrary"), vmem_limit_bytes=VMEM_LIMIT),
        name="pack_w_in",
    )(starts, jnp.swapaxes(w, 1, 2))


def _inproj_kernel(x_ref, g_ref, sh_ref, sc_ref, w_ref, ones_ref, qn_ref, kn_ref, cw_ref,
                   za0_ref, za1_ref, za2_ref, dq_ref, dk_ref, dv_ref, gate_ref, mg_ref,
                   ab_ref, abt_ref, carry_ref, dscr_ref, *, tm, cn):
    x = x_ref[0]
    hb = _norm_mod(x, g_ref[...], sh_ref[0], sc_ref[0]).astype(BF16)

    @pl.when(pl.program_id(1) == 0)
    def _():
        carry_ref[...] = jnp.zeros(carry_ref.shape, F32)

    jobs = []
    gw = ATT_GROUP_W

    def att_store(za_ref, g, part, val):
        dil = DILATIONS[g]
        if dil == 1:
            za_ref[0, :, part * gw:(part + 1) * gw] = val.astype(BF16)
            return
        for s in range(gw // LANE):
            dscr_ref[s] = val[:, s * LANE:(s + 1) * LANE]
        for r in range(dil):
            for s in range(gw // LANE):
                lo = r * 3 * gw + part * gw + s * LANE
                za_ref[0, :, lo:lo + LANE] = dscr_ref[s, pl.ds(r, tm // dil, stride=dil), :].astype(BF16)

    def att_norm(za_ref, g, part, gain_ref, mult):
        def epilogue(z):
            ms = _dot((z * z).astype(BF16), ones_ref[...])
            att_store(za_ref, g, part, (z * lax.rsqrt(ms + EPS)) * (gain_ref[...] * mult))
        return epilogue

    def att_plain(za_ref, g):
        def epilogue(z):
            att_store(za_ref, g, 2, z)
        return epilogue

    light = []
    for g, za_ref in enumerate((za0_ref, za1_ref, za2_ref)):
        c0 = P_ATT + g * 3 * gw
        light.append((c0, gw, att_norm(za_ref, g, 0, qn_ref, ATT_HEAD_DIM ** -0.5)))
        light.append((c0 + gw, gw, att_norm(za_ref, g, 1, kn_ref, 1.0)))
        light.append((c0 + 2 * gw, gw, att_plain(za_ref, g)))

    def dn_conv(out_ref, part, j):
        dc = part * DN_W + j * cn

        def epilogue(z):
            xe = jnp.concatenate([carry_ref[:, dc:dc + cn], z], axis=0)
            carry_ref[:, dc:dc + cn] = z[tm - SUBLANE:tm, :]
            y = cw_ref[CONV_W - 1:CONV_W, dc:dc + cn] * z
            for s in range(1, CONV_W):
                shifted = pltpu.roll(xe, s, axis=0)[SUBLANE:, :]
                y = y + cw_ref[CONV_W - 1 - s:CONV_W - s, dc:dc + cn] * shifted
            y = _silu(y)
            if part < 2:
                mult = DN_HEAD_DIM ** -0.5 if part == 0 else 1.0
                for h in range(cn // DN_HEAD_DIM):
                    yh = y[:, h * DN_HEAD_DIM:(h + 1) * DN_HEAD_DIM]
                    ss = jnp.sum(yh * yh, axis=-1, keepdims=True)
                    yn = yh * (lax.rsqrt(ss + EPS) * mult)
                    lo = j * cn + h * DN_HEAD_DIM
                    out_ref[0, :, lo:lo + DN_HEAD_DIM] = yn.astype(BF16)
            else:
                out_ref[0, :, j * cn:(j + 1) * cn] = y.astype(BF16)
        return epilogue

    heavy = []
    for part, out_ref in enumerate((dq_ref, dk_ref, dv_ref)):
        for j in range(DN_W // cn):
            heavy.append((P_DN + part * DN_W + j * cn, cn, dn_conv(out_ref, part, j)))

    def store_cast(out_ref, j, fn):
        def epilogue(z):
            out_ref[0, :, j * cn:(j + 1) * cn] = fn(z).astype(BF16)
        return epilogue

    for j in range(DN_W // cn):
        light.append((P_GATE + j * cn, cn, store_cast(gate_ref, j, lambda z: z)))
    for j in range(2 * D_MODEL // cn):
        light.append((P_MERGE + j * cn, cn, store_cast(mg_ref, j, _sigmoid)))

    def ab_epilogue(z):
        ab_ref[0] = z[:, :2 * DN_HEADS]
        abt_ref[0] = z.T[:2 * DN_HEADS, :]

    light.append((P_AB, 128, ab_epilogue))

    per_heavy = -(-len(light) // len(heavy))
    for i, job in enumerate(heavy):
        jobs.append(job)
        jobs.extend(light[i * per_heavy:(i + 1) * per_heavy])

    def project(job):
        return _dot(hb, w_ref[:, job[0]:job[0] + job[1]])

    pending = [project(job) for job in jobs[:INPROJ_LOOKAHEAD]]
    for i, job in enumerate(jobs):
        if i + INPROJ_LOOKAHEAD < len(jobs):
            pending.append(project(jobs[i + INPROJ_LOOKAHEAD]))
        job[2](pending.pop(0))


def _inproj(x, mod_l, gain, w_packed, layer, ones_blk, qn4, kn4, conv_w, tm=512, cn=512):
    bsz, t, d = x.shape
    bf = lambda n: jax.ShapeDtypeStruct((bsz, t, n), BF16)
    row = lambda n: pl.BlockSpec((1, tm, n), lambda b, i: (b, i, 0))
    att_spec = lambda dil: pl.BlockSpec((1, tm // dil, dil * 3 * ATT_GROUP_W), lambda b, i: (b, i, 0))
    att_shape = lambda dil: jax.ShapeDtypeStruct((bsz, t // dil, dil * 3 * ATT_GROUP_W), BF16)
    return pl.pallas_call(
        functools.partial(_inproj_kernel, tm=tm, cn=cn),
        grid=(bsz, t // tm),
        in_specs=[
            row(d),
            _const_spec((1, d), single_buffer=False),
            _mod_spec(3), _mod_spec(4),
            _layer_spec((d, P_TOTAL), layer),
            _const_spec((ATT_GROUP_W, ATT_GROUP_W)),
            _const_spec((1, ATT_GROUP_W), single_buffer=False),
            _const_spec((1, ATT_GROUP_W), single_buffer=False),
            _const_spec((CONV_W, 3 * DN_W), single_buffer=False),
        ],
        out_specs=[att_spec(dil) for dil in DILATIONS] + [row(DN_W)] * 4 + [row(2 * D_MODEL)]
        + [row(2 * DN_HEADS), pl.BlockSpec((1, 2 * DN_HEADS, tm), lambda b, i: (b, 0, i))],
        out_shape=[att_shape(dil) for dil in DILATIONS] + [bf(DN_W)] * 4 + [bf(2 * D_MODEL)]
        + [jax.ShapeDtypeStruct((bsz, t, 2 * DN_HEADS), F32),
           jax.ShapeDtypeStruct((bsz, 2 * DN_HEADS, t), F32)],
        scratch_shapes=[pltpu.VMEM((SUBLANE, 3 * DN_W), F32),
                        pltpu.VMEM((ATT_GROUP_W // LANE, tm, LANE), F32)],
        compiler_params=pltpu.CompilerParams(
            dimension_semantics=("arbitrary", "arbitrary"), vmem_limit_bytes=VMEM_LIMIT),
        name="inproj",
    )(x, gain.reshape(1, d), mod_l, mod_l, w_packed, ones_blk, qn4, kn4, conv_w)


def _attn_scores(q, kk, bias, head_masks):
    zero = jnp.zeros_like(q)
    qs = jnp.concatenate([jnp.where(hm, q, zero) for hm in head_masks], axis=0)
    return _dot_nt(qs, kk) + jnp.concatenate([bias] * ATT_HEADS, axis=0)


def _attn_probs(s):
    m = jnp.max(s, axis=-1, keepdims=True)
    p = jnp.exp(s - m)
    l = jnp.sum(p, axis=-1, keepdims=True)
    return p.astype(BF16), 1.0 / l, m + jnp.log(l)


def _attn_out(pv, inv, lse, head_masks):
    nq = pv.shape[0] // ATT_HEADS
    o = jnp.zeros((nq, ATT_GROUP_W), F32)
    ls = jnp.zeros((nq, ATT_GROUP_W), F32)
    for h, hm in enumerate(head_masks):
        rows = slice(h * nq, (h + 1) * nq)
        o = jnp.where(hm, pv[rows] * inv[rows], o)
        ls = jnp.where(hm, jnp.broadcast_to(lse[rows], (nq, ATT_GROUP_W)), ls)
    return o, ls


def _attn_kernel(q1_ref, k1_ref, v1_ref, k1p_ref, v1p_ref, q2_ref, k2_ref, v2_ref, z3_ref,
                 o1_ref, l1_ref, o2_ref, l2_ref, o3_ref, l3_ref):
    nb = ATT_BLOCK
    gw = ATT_GROUP_W
    step = pl.program_id(1)
    lane = lax.broadcasted_iota(jnp.int32, (1, gw), 1)
    head_masks = [(lane // ATT_HEAD_DIM) == h for h in range(ATT_HEADS)]
    qi = lax.broadcasted_iota(jnp.int32, (nb, 2 * nb), 0)
    kj = lax.broadcasted_iota(jnp.int32, (nb, 2 * nb), 1)
    ok_two = (kj >= qi) & (kj <= qi + nb)
    ok_cur = (kj >= nb) & (kj <= qi + nb)
    bias_two = jnp.where(ok_two, 0.0, NEG_BIG).astype(F32)
    bias_first = jnp.where(ok_cur, 0.0, NEG_BIG).astype(F32)
    bias_one = bias_first[:, nb:]

    def blocks(i):
        return slice(i * nb, (i + 1) * nb), slice((i - 1) * nb, (i + 1) * nb)

    def group0():
        jobs = []
        for i in range(4):
            rows, keys = blocks(i)
            if i == 0:
                kk = jnp.concatenate([k1p_ref[0], k1_ref[0, rows]], axis=0)
                vv = jnp.concatenate([v1p_ref[0], v1_ref[0, rows]], axis=0)
                bias = jnp.where(step == 0, bias_first, bias_two)
            else:
                kk, vv, bias = k1_ref[0, keys], v1_ref[0, keys], bias_two
            jobs.append((q1_ref[0, rows], kk, vv, bias, (o1_ref, l1_ref, rows, slice(None))))
        return jobs

    def group1():
        jobs = []
        for i in range(4):
            rows, keys = blocks(i)
            if i == 0:
                kk, vv, bias = k2_ref[0, rows], v2_ref[0, rows], bias_one
            else:
                kk, vv, bias = k2_ref[0, keys], v2_ref[0, keys], bias_two
            jobs.append((q2_ref[0, rows], kk, vv, bias, (o2_ref, l2_ref, rows, slice(None))))
        return jobs

    def group2():
        jobs = []
        for r in range(4):
            c0 = r * 3 * gw
            jobs.append((z3_ref[0, :, c0:c0 + gw], z3_ref[0, :, c0 + gw:c0 + 2 * gw],
                         z3_ref[0, :, c0 + 2 * gw:c0 + 3 * gw], bias_one,
                         (o3_ref, l3_ref, slice(None), slice(r * gw, (r + 1) * gw))))
        return jobs

    def scores(jobs):
        return [_attn_scores(q, kk, bias, head_masks) for q, kk, _, bias, _ in jobs]

    def finish(jobs, ss):
        probs = [_attn_probs(s) for s in ss]
        pvs = [_dot(p, job[2]) for (p, _, _), job in zip(probs, jobs)]
        for pv, (_, inv, lse), job in zip(pvs, probs, jobs):
            o_ref, l_ref, rows, cols = job[4]
            o, ls = _attn_out(pv, inv, lse, head_masks)
            o_ref[0, rows, cols] = o
            l_ref[0, rows, cols] = ls

    j0, j1, j2 = group0(), group1(), group2()
    s0 = scores(j0)
    s1 = scores(j1)
    finish(j0, s0)
    s2 = scores(j2)
    finish(j1, s1)
    finish(j2, s2)


def _attention(za0, z2, z3):
    bsz, t, _ = za0.shape
    gw, nb = ATT_GROUP_W, ATT_BLOCK
    steps = 4
    rows1 = t // steps
    assert rows1 == 4 * nb and t // DILATIONS[1] == 4 * nb and t // DILATIONS[2] == nb
    prev = lambda b, j: (b, jnp.maximum(4 * j - 1, 0), 1)
    prev_v = lambda b, j: (b, jnp.maximum(4 * j - 1, 0), 2)
    in_specs = [
        pl.BlockSpec((1, rows1, gw), lambda b, j: (b, j, 0)),
        pl.BlockSpec((1, rows1, gw), lambda b, j: (b, j, 1)),
        pl.BlockSpec((1, rows1, gw), lambda b, j: (b, j, 2)),
        pl.BlockSpec((1, nb, gw), prev),
        pl.BlockSpec((1, nb, gw), prev_v),
        pl.BlockSpec((1, 4 * nb, gw), lambda b, j: (b, 0, 3 * j)),
        pl.BlockSpec((1, 4 * nb, gw), lambda b, j: (b, 0, 3 * j + 1)),
        pl.BlockSpec((1, 4 * nb, gw), lambda b, j: (b, 0, 3 * j + 2)),
        pl.BlockSpec((1, nb, 4 * 3 * gw), lambda b, j: (b, 0, j)),
    ]
    o1s = pl.BlockSpec((1, rows1, gw), lambda b, j: (b, j, 0))
    o2s = pl.BlockSpec((1, 4 * nb, gw), lambda b, j: (b, 0, j))
    o3s = pl.BlockSpec((1, nb, 4 * gw), lambda b, j: (b, 0, j))
    s1 = jax.ShapeDtypeStruct((bsz, t, gw), F32)
    s2 = jax.ShapeDtypeStruct((bsz, t // DILATIONS[1], DILATIONS[1] * gw), F32)
    s3 = jax.ShapeDtypeStruct((bsz, t // DILATIONS[2], DILATIONS[2] * gw), F32)
    o1, l1, o2, l2, o3, l3 = pl.pallas_call(
        _attn_kernel,
        grid=(bsz, steps),
        in_specs=in_specs,
        out_specs=[o1s, o1s, o2s, o2s, o3s, o3s],
        out_shape=[s1, s1, s2, s2, s3, s3],
        compiler_params=pltpu.CompilerParams(
            dimension_semantics=("arbitrary", "arbitrary"), vmem_limit_bytes=VMEM_LIMIT),
        name="dilated_attn",
    )(za0, za0, za0, za0, za0, z2, z2, z2, z3)
    return o1, l1, o2, l2, o3, l3


def _softplus(x):
    return jnp.maximum(x, 0.0) + jnp.log(1.0 + jnp.exp(-jnp.abs(x)))


def _chunk_masks():
    c = DN_CHUNK
    ii = lax.broadcasted_iota(jnp.int32, (c, c), 0)
    jj = lax.broadcasted_iota(jnp.int32, (c, c), 1)
    return ii >= jj, ii > jj, ii <= jj


def _chunk_decay(ab, abt, alog_ref, dtb_ref, alog_c_ref, dtb_c_ref, tril, triu):
    g_col = -jnp.exp(alog_ref[...]) * _softplus(ab + dtb_ref[...])
    g_row = -jnp.exp(alog_c_ref[...]) * _softplus(abt + dtb_c_ref[...])
    gc_col = _dot_exact(jnp.where(tril, 1.0, 0.0).astype(F32), g_col)
    gc_row = _dot_exact(g_row, jnp.where(triu, 1.0, 0.0).astype(F32))
    return gc_col, gc_row, _sigmoid(ab)


def _head_decay(gc_col, gc_row, beta_col, h, tril):
    c, hd = DN_CHUNK, DN_HEAD_DIM
    cb = jnp.broadcast_to(gc_col[:, h:h + 1], (c, hd))
    bb = jnp.broadcast_to(beta_col[:, DN_HEADS + h:DN_HEADS + h + 1], (c, hd))
    rb = jnp.broadcast_to(gc_row[h:h + 1, :], (c, c))
    ld = jnp.exp(jnp.where(tril, cb[:, :c] - rb, 0.0))
    return cb, bb, ld


def _dn_prep_kernel(k_ref, ab_ref, abt_ref, alog_ref, dtb_ref, alog_c_ref, dtb_c_ref, a_ref,
                    gcc_ref, gcr_ref, *, n_chunks):
    c, hd = DN_CHUNK, DN_HEAD_DIM
    tril, strict, triu = _chunk_masks()
    lane16 = lax.broadcasted_iota(jnp.int32, (c, 2 * DN_HEADS), 1)
    for n in range(n_chunks):
        r0 = n * c
        gc_col, gc_row, beta_col = _chunk_decay(ab_ref[0, r0:r0 + c, :], abt_ref[0, n], alog_ref,
                                                dtb_ref, alog_c_ref, dtb_c_ref, tril, triu)
        gcc_ref[0, r0:r0 + c, :] = jnp.where(lane16 < DN_HEADS, gc_col, beta_col)
        gcr_ref[0, n] = gc_row
        for h in range(DN_HEADS):
            k = k_ref[0, r0:r0 + c, h * hd:(h + 1) * hd]
            _, bb, ld = _head_decay(gc_col, gc_row, beta_col, h, tril)
            a_ref[0, n, h] = jnp.where(strict, _dot_nt(k, k) * (bb[:, :c] * ld), 0.0)


def _dn_solve_kernel(a_ref, t_ref, at_scr, tt_scr):
    c = DN_CHUNK
    tiles = c * c // LANE
    per = LANE // c
    for t in range(tiles):
        blk = a_ref[:, t * LANE:(t + 1) * LANE].T
        for r in range(per):
            at_scr[t * per + r] = blk[r * c:(r + 1) * c]
    row_id = lax.broadcasted_iota(jnp.int32, (SUBLANE, LANE), 0)
    zero = jnp.zeros((SUBLANE, LANE), F32)
    for i in range(c):
        groups = i // SUBLANE + 1
        acc = [zero] * (c // SUBLANE)
        acc[groups - 1] = jnp.where(row_id == i % SUBLANE, 1.0, 0.0).astype(F32)
        for j in range(i):
            a = jnp.broadcast_to(at_scr[i, j:j + 1, :], (SUBLANE, LANE))
            for g in range(j // SUBLANE + 1):
                acc[g] = acc[g] - a * tt_scr[j, g * SUBLANE:(g + 1) * SUBLANE, :]
        for g in range(c // SUBLANE):
            tt_scr[i, g * SUBLANE:(g + 1) * SUBLANE, :] = acc[g]
    for t in range(tiles):
        blk = jnp.concatenate([tt_scr[t * per + r] for r in range(per)], axis=0)
        t_ref[:, t * LANE:(t + 1) * LANE] = blk.T.astype(t_ref.dtype)


def _dn_scan_kernel(q_ref, k_ref, v_ref, gcc_ref, gcr_ref, t_ref, o_ref, s_ref, *, n_chunks):
    c, hd = DN_CHUNK, DN_HEAD_DIM

    @pl.when(pl.program_id(1) == 0)
    def _():
        s_ref[...] = jnp.zeros(s_ref.shape, F32)

    tril, _, _ = _chunk_masks()
    heads = range(DN_HEADS)

    def stage_a(n):
        r0 = n * c
        gc_col = beta_col = gcc_ref[0, r0:r0 + c, :]
        gc_row = gcr_ref[0, n]
        dec, qk, xs, egc = [], [], [], []
        for h in heads:
            dec.append(_head_decay(gc_col, gc_row, beta_col, h, tril))
        for h in heads:
            lanes = slice(h * hd, (h + 1) * hd)
            qk.append(_dot_nt(q_ref[0, r0:r0 + c, lanes], k_ref[0, r0:r0 + c, lanes]))
        for h in heads:
            lanes = slice(h * hd, (h + 1) * hd)
            cb, bb, _ = dec[h]
            eb = jnp.exp(cb)
            egc.append(eb.astype(BF16))
            rhs = jnp.concatenate([v_ref[0, r0:r0 + c, lanes] * bb.astype(BF16),
                                   k_ref[0, r0:r0 + c, lanes] * (bb * eb).astype(BF16)], axis=1)
            xs.append(_dot(t_ref[0, n, h], rhs))
        per_head = []
        for h in heads:
            lanes = slice(h * hd, (h + 1) * hd)
            cb, _, ld = dec[h]
            k = k_ref[0, r0:r0 + c, lanes].astype(F32)
            gl = cb[c - 1:c, :]
            attn = jnp.where(tril, qk[h] * ld, 0.0).astype(BF16)
            kdec_t = (k * jnp.exp(gl - cb)).T.astype(BF16)
            lhs_s = jnp.concatenate([xs[h][:, hd:].astype(BF16),
                                     q_ref[0, r0:r0 + c, lanes] * egc[h]], axis=0)
            per_head.append((xs[h][:, :hd], lhs_s, jnp.concatenate([attn, kdec_t], axis=0),
                             jnp.exp(gl)))
        return per_head

    state = [s_ref[h] for h in heads]
    cur = stage_a(0)
    for n in range(n_chunks):
        r0 = n * c
        ws = [_dot(cur[h][1], state[h].astype(BF16)) for h in heads]
        nxt = stage_a(n + 1) if n + 1 < n_chunks else None
        vn = [(cur[h][0] - ws[h][:c]).astype(BF16) for h in heads]
        upd = [_dot(cur[h][2], vn[h]) for h in heads]
        for h in heads:
            lanes = slice(h * hd, (h + 1) * hd)
            o_ref[0, r0:r0 + c, lanes] = (ws[h][c:] + upd[h][:c]).astype(BF16)
            state[h] = state[h] * cur[h][3] + upd[h][c:]
        cur = nxt
    for h in heads:
        s_ref[h] = state[h]


def _deltanet(dq, dk, dv, ab, abt, a_log, dt_bias, rows=256):
    bsz, t, w = dq.shape
    c = DN_CHUNK
    n_chunks = rows // c
    total_chunks = t // c
    abt_c = abt.reshape(bsz, 2 * DN_HEADS, total_chunks, c).transpose(0, 2, 1, 3)
    pad = jnp.zeros((DN_HEADS,), F32)
    alog16 = jnp.concatenate([a_log, pad])
    dtb16 = jnp.concatenate([dt_bias, pad])
    small = [alog16.reshape(1, -1), dtb16.reshape(1, -1), alog16.reshape(-1, 1), dtb16.reshape(-1, 1)]
    small_specs = [_const_spec((1, 2 * DN_HEADS), single_buffer=False)] * 2 + \
                  [_const_spec((2 * DN_HEADS, 1), single_buffer=False)] * 2
    row = pl.BlockSpec((1, rows, w), lambda b, i: (b, i, 0))
    ab_spec = pl.BlockSpec((1, rows, 2 * DN_HEADS), lambda b, i: (b, i, 0))
    abt_spec = pl.BlockSpec((1, n_chunks, 2 * DN_HEADS, c), lambda b, i: (b, i, 0, 0))
    mat_spec = pl.BlockSpec((1, n_chunks, DN_HEADS, c, c), lambda b, i: (b, i, 0, 0, 0))
    mat_shape = jax.ShapeDtypeStruct((bsz, total_chunks, DN_HEADS, c, c), F32)
    params = pltpu.CompilerParams(dimension_semantics=("arbitrary", "arbitrary"),
                                  vmem_limit_bytes=VMEM_LIMIT)

    a_mat, gcc, gcr = pl.pallas_call(
        functools.partial(_dn_prep_kernel, n_chunks=n_chunks),
        grid=(bsz, t // rows),
        in_specs=[row, ab_spec, abt_spec] + small_specs,
        out_specs=[mat_spec, ab_spec, abt_spec],
        out_shape=[mat_shape, jax.ShapeDtypeStruct(ab.shape, F32),
                   jax.ShapeDtypeStruct(abt_c.shape, F32)],
        compiler_params=params,
        name="dn_prep",
    )(dk, ab, abt_c, *small)

    n_sys = bsz * total_chunks * DN_HEADS
    assert n_sys % LANE == 0
    flat_spec = pl.BlockSpec((LANE, c * c), lambda i: (i, 0))
    t_mat = pl.pallas_call(
        _dn_solve_kernel,
        grid=(n_sys // LANE,),
        in_specs=[flat_spec],
        out_specs=flat_spec,
        out_shape=jax.ShapeDtypeStruct((n_sys, c * c), BF16),
        scratch_shapes=[pltpu.VMEM((c, c, LANE), F32), pltpu.VMEM((c, c, LANE), F32)],
        compiler_params=pltpu.CompilerParams(dimension_semantics=("arbitrary",),
                                             vmem_limit_bytes=VMEM_LIMIT),
        name="dn_solve",
    )(a_mat.reshape(n_sys, c * c)).reshape(mat_shape.shape)

    return pl.pallas_call(
        functools.partial(_dn_scan_kernel, n_chunks=n_chunks),
        grid=(bsz, t // rows),
        in_specs=[row, row, row, ab_spec, abt_spec, mat_spec],
        out_specs=row,
        out_shape=jax.ShapeDtypeStruct((bsz, t, w), BF16),
        scratch_shapes=[pltpu.VMEM((DN_HEADS, DN_HEAD_DIM, DN_HEAD_DIM), F32)],
        compiler_params=params,
        name="dn_scan",
    )(dq, dk, dv, gcc, gcr, t_mat)


def _merge_kernel(o1_ref, l1_ref, o2_ref, l2_ref, o3_ref, l3_ref, odn_ref, gate_ref, nrm_ref,
                  mg_ref, x_ref, gt_ref, wpa_ref, wpd_ref, wo_ref, out_ref, nat_scr):
    d = D_MODEL
    tm = x_ref.shape[1]
    slabs = ATT_GROUP_W // LANE
    for a, (ref, dil) in enumerate(((o2_ref, DILATIONS[1]), (l2_ref, DILATIONS[1]),
                                    (o3_ref, DILATIONS[2]), (l3_ref, DILATIONS[2]))):
        for r in range(dil):
            for s in range(slabs):
                lo = r * ATT_GROUP_W + s * LANE
                nat_scr[a, s, pl.ds(r, tm // dil, stride=dil), :] = ref[0, :, lo:lo + LANE]
    halves = [slice(i * (tm // 2), (i + 1) * (tm // 2)) for i in range(2)]
    yd = []
    for r in halves:
        gated = []
        for h in range(DN_HEADS):
            lanes = slice(h * DN_HEAD_DIM, (h + 1) * DN_HEAD_DIM)
            o = odn_ref[0, r, lanes].astype(F32)
            ms = jnp.mean(o * o, axis=-1, keepdims=True)
            on = (o * lax.rsqrt(ms + EPS)) * nrm_ref[...]
            gated.append((on * _silu(gate_ref[0, r, lanes].astype(F32))).astype(BF16))
        yd.append(_dot(jnp.concatenate(gated, axis=1), wpd_ref[...]))
    ya = []
    for r in halves:
        ys = []
        for s in range(slabs):
            cols = slice(s * LANE, (s + 1) * LANE)
            l1, l2, l3 = l1_ref[0, r, cols], nat_scr[1, s, r, :], nat_scr[3, s, r, :]
            m = jnp.maximum(jnp.maximum(l1, l2), l3)
            e1, e2, e3 = jnp.exp(l1 - m), jnp.exp(l2 - m), jnp.exp(l3 - m)
            y = e1 * o1_ref[0, r, cols] + e2 * nat_scr[0, s, r, :] + e3 * nat_scr[2, s, r, :]
            ys.append((y / (e1 + e2 + e3)).astype(BF16))
        ya.append(_dot(jnp.concatenate(ys, axis=1), wpa_ref[...]))
    for i, r in enumerate(halves):
        mix = mg_ref[0, r, :d].astype(F32) * ya[i] + mg_ref[0, r, d:].astype(F32) * yd[i]
        out = _dot(mix.astype(BF16), wo_ref[...])
        out_ref[0, r] = x_ref[0, r] + gt_ref[0] * out


def _merge(att, odn, gate, dn_norm, mg, x, mod_l, wpa, wpd, wo, layer, tm=512):
    bsz, t, d = x.shape
    row = lambda n: pl.BlockSpec((1, tm, n), lambda b, i: (b, i, 0))
    view = lambda dil: pl.BlockSpec((1, tm // dil, dil * ATT_GROUP_W), lambda b, i: (b, i, 0))
    att_specs = [row(ATT_GROUP_W)] * 2 + [view(DILATIONS[1])] * 2 + [view(DILATIONS[2])] * 2
    return pl.pallas_call(
        _merge_kernel,
        grid=(bsz, t // tm),
        in_specs=att_specs + [row(DN_W), row(DN_W), _const_spec((1, DN_HEAD_DIM), single_buffer=False),
                              row(2 * d), row(d), _mod_spec(5),
                              _layer_spec(wpa.shape[1:], layer), _layer_spec(wpd.shape[1:], layer),
                              _layer_spec(wo.shape[1:], layer)],
        out_specs=row(d),
        out_shape=jax.ShapeDtypeStruct(x.shape, F32),
        scratch_shapes=[pltpu.VMEM((4, ATT_GROUP_W // LANE, tm, LANE), F32)],
        compiler_params=pltpu.CompilerParams(
            dimension_semantics=("arbitrary", "arbitrary"), vmem_limit_bytes=VMEM_LIMIT),
        name="merge_out",
    )(*att, odn, gate, dn_norm.reshape(1, -1), mg, x, mod_l, wpa, wpd, wo)


def _mixer(x, mod_l, layer, gain, w_packed, ones_blk, qn4, kn4, conv_w, a_log, dt_bias, dn_norm,
           wpa, wpd, wo):
    za0, za1, za2, dq, dk, dv, gate, mg, ab, abt = _inproj(
        x, mod_l, gain, w_packed, layer, ones_blk, qn4, kn4, conv_w)
    att = _attention(za0, za1, za2)
    odn = _deltanet(dq, dk, dv, ab, abt, a_log, dt_bias)
    return _merge(att, odn, gate, dn_norm, mg, x, mod_l, wpa, wpd, wo, layer)


def kernel(x, c, ada_w, ada_b, norm_ff1, ffn1_w_up, ffn1_w_down, norm_mix, w_in, q_norm, k_norm,
           conv_w, a_log, dt_bias, dn_norm, w_proj_att, w_proj_dn, w_out, norm_ff2, ffn2_w_up,
           ffn2_w_down):
    bsz = x.shape[0]
    mod = _ada_mod(c, ada_w, ada_b)
    blk = jnp.arange(ATT_GROUP_W) // ATT_HEAD_DIM
    ones_blk = jnp.where(blk[:, None] == blk[None, :], 1.0 / ATT_HEAD_DIM, 0.0).astype(BF16)
    w_in_packed = _pack_w_in(w_in)
    up1, dn1, up2, dn2 = (w.astype(BF16) for w in (ffn1_w_up, ffn1_w_down, ffn2_w_up, ffn2_w_down))
    wpa, wpd, wo = (w.astype(BF16) for w in (w_proj_att, w_proj_dn, w_out))
    for l in range(DEPTH):
        mod_l = mod[l].reshape(bsz, 1, N_ADA * D_MODEL)
        x = _ffn(x, mod_l, 0, norm_ff1[l], up1, dn1, l)
        x = _mixer(x, mod_l, l, norm_mix[l], w_in_packed, ones_blk,
                   jnp.tile(q_norm[l], ATT_HEADS).reshape(1, -1),
                   jnp.tile(k_norm[l], ATT_HEADS).reshape(1, -1),
                   conv_w[l], a_log[l], dt_bias[l], dn_norm[l], wpa, wpd, wo)
        x = _ffn(x, mod_l, 6, norm_ff2[l], up2, dn2, l)
    return x
```

```python
import functools

import jax
import jax.numpy as jnp
from jax import lax
from jax.experimental import pallas as pl
from jax.experimental.pallas import tpu as pltpu

F32 = jnp.float32
BF16 = jnp.bfloat16

D_MODEL = 1024
DEPTH = 4
SEQ = 2048
D_FF = 2816
EPS = 1e-6
N_ADA = 9

ATT_HEAD_DIM = 64
ATT_HEADS = 4
ATT_GROUP_W = ATT_HEADS * ATT_HEAD_DIM
DILATIONS = (1, 4, 16)
ATT_BLOCK = 128
N_GROUPS = 3
ATT_W = N_GROUPS * ATT_GROUP_W

DN_HEADS = 8
DN_HEAD_DIM = 128
DN_W = DN_HEADS * DN_HEAD_DIM
DN_CHUNK = 64
CONV_W = 4

OFF_DN_QKV = 3 * ATT_W
OFF_DN_GATE = OFF_DN_QKV + 3 * DN_W
OFF_DN_A = OFF_DN_GATE + DN_W
OFF_DN_B = OFF_DN_A + DN_HEADS
OFF_MERGE = OFF_DN_B + DN_HEADS
N_IN = OFF_MERGE + 2 * D_MODEL

P_ATT = 0
P_DN = P_ATT + 3 * ATT_W
P_GATE = P_DN + 3 * DN_W
P_MERGE = P_GATE + DN_W
P_AB = P_MERGE + 2 * D_MODEL
P_TOTAL = P_AB + 256

LANE = 128
SUBLANE = 8
NEG_BIG = -1e30
VMEM_LIMIT = 56 * 1024 * 1024
INPROJ_LOOKAHEAD = 1


def _dot(a, b):
    return jnp.dot(a, b, preferred_element_type=F32)


def _dot_nt(a, b):
    return lax.dot_general(a, b, (((1,), (1,)), ((), ())), preferred_element_type=F32)


def _dot_exact(a, b):
    return jnp.dot(a, b, preferred_element_type=F32, precision=lax.Precision.HIGHEST)


def _sigmoid(x):
    return 1.0 / (1.0 + jnp.exp(-x))


def _silu(x):
    return x * _sigmoid(x)


def _norm_mod(x, gain, shift, scale):
    ms = jnp.mean(x * x, axis=-1, keepdims=True)
    y = (x * lax.rsqrt(ms + EPS)) * gain
    return y * (1.0 + scale) + shift


def _const_spec(shape, single_buffer=True):
    nd = len(shape)
    kwargs = {"pipeline_mode": pl.Buffered(1)} if single_buffer else {}
    return pl.BlockSpec(shape, lambda *_: (0,) * nd, **kwargs)


def _layer_spec(shape, layer):
    nd = len(shape)
    return pl.BlockSpec((None,) + tuple(shape), lambda *_: (layer,) + (0,) * nd,
                        pipeline_mode=pl.Buffered(1))


def _mod_spec(k):
    return pl.BlockSpec((1, 1, D_MODEL), lambda b, i: (b, 0, k))


def _ada_kernel(c_ref, w_ref, b_ref, o_ref):
    c = c_ref[...]
    ca = _silu(c).astype(BF16)
    o_ref[0] = _dot(ca, w_ref[0].astype(BF16)) + b_ref[0]


def _ada_mod(c, ada_w, ada_b, tn=1024):
    depth, d, n = ada_w.shape
    bsz = c.shape[0]
    return pl.pallas_call(
        _ada_kernel,
        grid=(depth, n // tn),
        in_specs=[
            pl.BlockSpec((bsz, d), lambda l, j: (0, 0)),
            pl.BlockSpec((1, d, tn), lambda l, j: (l, 0, j)),
            pl.BlockSpec((1, 1, tn), lambda l, j: (l, 0, j)),
        ],
        out_specs=pl.BlockSpec((1, bsz, tn), lambda l, j: (l, 0, j)),
        out_shape=jax.ShapeDtypeStruct((depth, bsz, n), F32),
        name="ada_mod",
    )(c, ada_w, ada_b.reshape(depth, 1, n))


def _ffn_kernel(x_ref, g_ref, sh_ref, sc_ref, gt_ref, wup_ref, wdn_ref, o_ref, *, tf):
    x = x_ref[0]
    hb = _norm_mod(x, g_ref[...], sh_ref[0], sc_ref[0]).astype(BF16)
    acc = jnp.zeros(x.shape, F32)
    for f in range(D_FF // tf):
        gate = _dot(hb, wup_ref[:, f * tf:(f + 1) * tf])
        up = _dot(hb, wup_ref[:, D_FF + f * tf:D_FF + (f + 1) * tf])
        a = (_silu(gate) * up).astype(BF16)
        acc = acc + _dot(a, wdn_ref[f * tf:(f + 1) * tf, :])
    o_ref[0] = x + (0.5 * gt_ref[0]) * acc


def _ffn(x, mod_l, k0, gain, wup, wdn, layer, tm=512, tf=256):
    bsz, t, d = x.shape
    return pl.pallas_call(
        functools.partial(_ffn_kernel, tf=tf),
        grid=(bsz, t // tm),
        in_specs=[
            pl.BlockSpec((1, tm, d), lambda b, i: (b, i, 0)),
            _const_spec((1, d), single_buffer=False),
            _mod_spec(k0), _mod_spec(k0 + 1), _mod_spec(k0 + 2),
            _layer_spec((d, 2 * D_FF), layer),
            _layer_spec((D_FF, d), layer),
        ],
        out_specs=pl.BlockSpec((1, tm, d), lambda b, i: (b, i, 0)),
        out_shape=jax.ShapeDtypeStruct(x.shape, F32),
        compiler_params=pltpu.CompilerParams(
            dimension_semantics=("arbitrary", "arbitrary"), vmem_limit_bytes=VMEM_LIMIT),
        name="ffn",
    )(x, gain.reshape(1, d), mod_l, mod_l, mod_l, wup, wdn)


PACK_TILE = 256


def _pack_starts():
    gw = ATT_GROUP_W
    starts = [part * ATT_W + g * gw for g in range(N_GROUPS) for part in range(3)]
    starts += list(range(OFF_DN_QKV, OFF_DN_A, PACK_TILE))
    starts += list(range(OFF_MERGE, N_IN, PACK_TILE))
    starts.append(OFF_DN_A)
    assert len(starts) * PACK_TILE == P_TOTAL
    return starts


def _pack_w_in_kernel(starts_ref, w_ref, o_ref):
    del starts_ref
    t = w_ref[0].T
    last = pl.program_id(1) == pl.num_programs(1) - 1
    keep = jnp.where(last, 2 * DN_HEADS, PACK_TILE)
    lane = lax.broadcasted_iota(jnp.int32, (1, PACK_TILE), 1)
    o_ref[0] = jnp.where(lane < keep, t, 0.0).astype(BF16)


def _pack_w_in(w):
    depth, d, n = w.shape
    starts = jnp.asarray(_pack_starts(), jnp.int32)
    grid_spec = pltpu.PrefetchScalarGridSpec(
        num_scalar_prefetch=1,
        grid=(depth, P_TOTAL // PACK_TILE),
        in_specs=[pl.BlockSpec((pl.Element(1), pl.Element(PACK_TILE), pl.Element(d)),
                               lambda l, j, st: (l, pl.multiple_of(st[j], SUBLANE), 0))],
        out_specs=pl.BlockSpec((1, d, PACK_TILE), lambda l, j, st: (l, 0, j)),
    )
    return pl.pallas_call(
        _pack_w_in_kernel,
        grid_spec=grid_spec,
        out_shape=jax.ShapeDtypeStruct((depth, d, P_TOTAL), BF16),
        compiler_params=pltpu.CompilerParams(
            dimension_semantics=("arbitrary", "arbitrary"), vmem_limit_bytes=VMEM_LIMIT),
        name="pack_w_in",
    )(starts, jnp.swapaxes(w, 1, 2))


def _inproj_kernel(x_ref, g_ref, sh_ref, sc_ref, w_ref, ones_ref, qn_ref, kn_ref, cw_ref,
                   za0_ref, za1_ref, za2_ref, dq_ref, dk_ref, dv_ref, gate_ref, mg_ref,
                   ab_ref, abt_ref, carry_ref, dscr_ref, *, tm, cn):
    x = x_ref[0]
    hb = _norm_mod(x, g_ref[...], sh_ref[0], sc_ref[0]).astype(BF16)

    @pl.when(pl.program_id(1) == 0)
    def _():
        carry_ref[...] = jnp.zeros(carry_ref.shape, F32)

    jobs = []
    gw = ATT_GROUP_W

    def att_store(za_ref, g, part, val):
        dil = DILATIONS[g]
        if dil == 1:
            za_ref[0, :, part * gw:(part + 1) * gw] = val.astype(BF16)
            return
        for s in range(gw // LANE):
            dscr_ref[s] = val[:, s * LANE:(s + 1) * LANE]
        for r in range(dil):
            for s in range(gw // LANE):
                lo = r * 3 * gw + part * gw + s * LANE
                za_ref[0, :, lo:lo + LANE] = dscr_ref[s, pl.ds(r, tm // dil, stride=dil), :].astype(BF16)

    def att_norm(za_ref, g, part, gain_ref, mult):
        def epilogue(z):
            ms = _dot((z * z).astype(BF16), ones_ref[...])
            att_store(za_ref, g, part, (z * lax.rsqrt(ms + EPS)) * (gain_ref[...] * mult))
        return epilogue

    def att_plain(za_ref, g):
        def epilogue(z):
            att_store(za_ref, g, 2, z)
        return epilogue

    light = []
    for g, za_ref in enumerate((za0_ref, za1_ref, za2_ref)):
        c0 = P_ATT + g * 3 * gw
        light.append((c0, gw, att_norm(za_ref, g, 0, qn_ref, ATT_HEAD_DIM ** -0.5)))
        light.append((c0 + gw, gw, att_norm(za_ref, g, 1, kn_ref, 1.0)))
        light.append((c0 + 2 * gw, gw, att_plain(za_ref, g)))

    def dn_conv(out_ref, part, j):
        dc = part * DN_W + j * cn

        def epilogue(z):
            xe = jnp.concatenate([carry_ref[:, dc:dc + cn], z], axis=0)
            carry_ref[:, dc:dc + cn] = z[tm - SUBLANE:tm, :]
            y = cw_ref[CONV_W - 1:CONV_W, dc:dc + cn] * z
            for s in range(1, CONV_W):
                shifted = pltpu.roll(xe, s, axis=0)[SUBLANE:, :]
                y = y + cw_ref[CONV_W - 1 - s:CONV_W - s, dc:dc + cn] * shifted
            y = _silu(y)
            if part < 2:
                mult = DN_HEAD_DIM ** -0.5 if part == 0 else 1.0
                for h in range(cn // DN_HEAD_DIM):
                    yh = y[:, h * DN_HEAD_DIM:(h + 1) * DN_HEAD_DIM]
                    ss = jnp.sum(yh * yh, axis=-1, keepdims=True)
                    yn = yh * (lax.rsqrt(ss + EPS) * mult)
                    lo = j * cn + h * DN_HEAD_DIM
                    out_ref[0, :, lo:lo + DN_HEAD_DIM] = yn.astype(BF16)
            else:
                out_ref[0, :, j * cn:(j + 1) * cn] = y.astype(BF16)
        return epilogue

    heavy = []
    for part, out_ref in enumerate((dq_ref, dk_ref, dv_ref)):
        for j in range(DN_W // cn):
            heavy.append((P_DN + part * DN_W + j * cn, cn, dn_conv(out_ref, part, j)))

    def store_cast(out_ref, j, fn):
        def epilogue(z):
            out_ref[0, :, j * cn:(j + 1) * cn] = fn(z).astype(BF16)
        return epilogue

    for j in range(DN_W // cn):
        light.append((P_GATE + j * cn, cn, store_cast(gate_ref, j, lambda z: z)))
    for j in range(2 * D_MODEL // cn):
        light.append((P_MERGE + j * cn, cn, store_cast(mg_ref, j, _sigmoid)))

    def ab_epilogue(z):
        ab_ref[0] = z[:, :2 * DN_HEADS]
        abt_ref[0] = z.T[:2 * DN_HEADS, :]

    light.append((P_AB, 128, ab_epilogue))

    per_heavy = -(-len(light) // len(heavy))
    for i, job in enumerate(heavy):
        jobs.append(job)
        jobs.extend(light[i * per_heavy:(i + 1) * per_heavy])

    def project(job):
        return _dot(hb, w_ref[:, job[0]:job[0] + job[1]])

    pending = [project(job) for job in jobs[:INPROJ_LOOKAHEAD]]
    for i, job in enumerate(jobs):
        if i + INPROJ_LOOKAHEAD < len(jobs):
            pending.append(project(jobs[i + INPROJ_LOOKAHEAD]))
        job[2](pending.pop(0))


def _inproj(x, mod_l, gain, w_packed, layer, ones_blk, qn4, kn4, conv_w, tm=512, cn=512):
    bsz, t, d = x.shape
    bf = lambda n: jax.ShapeDtypeStruct((bsz, t, n), BF16)
    row = lambda n: pl.BlockSpec((1, tm, n), lambda b, i: (b, i, 0))
    att_spec = lambda dil: pl.BlockSpec((1, tm // dil, dil * 3 * ATT_GROUP_W), lambda b, i: (b, i, 0))
    att_shape = lambda dil: jax.ShapeDtypeStruct((bsz, t // dil, dil * 3 * ATT_GROUP_W), BF16)
    return pl.pallas_call(
        functools.partial(_inproj_kernel, tm=tm, cn=cn),
        grid=(bsz, t // tm),
        in_specs=[
            row(d),
            _const_spec((1, d), single_buffer=False),
            _mod_spec(3), _mod_spec(4),
            _layer_spec((d, P_TOTAL), layer),
            _const_spec((ATT_GROUP_W, ATT_GROUP_W)),
            _const_spec((1, ATT_GROUP_W), single_buffer=False),
            _const_spec((1, ATT_GROUP_W), single_buffer=False),
            _const_spec((CONV_W, 3 * DN_W), single_buffer=False),
        ],
        out_specs=[att_spec(dil) for dil in DILATIONS] + [row(DN_W)] * 4 + [row(2 * D_MODEL)]
        + [row(2 * DN_HEADS), pl.BlockSpec((1, 2 * DN_HEADS, tm), lambda b, i: (b, 0, i))],
        out_shape=[att_shape(dil) for dil in DILATIONS] + [bf(DN_W)] * 4 + [bf(2 * D_MODEL)]
        + [jax.ShapeDtypeStruct((bsz, t, 2 * DN_HEADS), F32),
           jax.ShapeDtypeStruct((bsz, 2 * DN_HEADS, t), F32)],
        scratch_shapes=[pltpu.VMEM((SUBLANE, 3 * DN_W), F32),
                        pltpu.VMEM((ATT_GROUP_W // LANE, tm, LANE), F32)],
        compiler_params=pltpu.CompilerParams(
            dimension_semantics=("arbitrary", "arbitrary"), vmem_limit_bytes=VMEM_LIMIT),
        name="inproj",
    )(x, gain.reshape(1, d), mod_l, mod_l, w_packed, ones_blk, qn4, kn4, conv_w)


def _attn_scores(q, kk, bias, head_masks):
    zero = jnp.zeros_like(q)
    qs = jnp.concatenate([jnp.where(hm, q, zero) for hm in head_masks], axis=0)
    return _dot_nt(qs, kk) + jnp.concatenate([bias] * ATT_HEADS, axis=0)


def _attn_probs(s):
    m = jnp.max(s, axis=-1, keepdims=True)
    p = jnp.exp(s - m)
    l = jnp.sum(p, axis=-1, keepdims=True)
    return p.astype(BF16), 1.0 / l, m + jnp.log(l)


def _attn_out(pv, inv, lse, head_masks):
    nq = pv.shape[0] // ATT_HEADS
    o = jnp.zeros((nq, ATT_GROUP_W), F32)
    ls = jnp.zeros((nq, ATT_GROUP_W), F32)
    for h, hm in enumerate(head_masks):
        rows = slice(h * nq, (h + 1) * nq)
        o = jnp.where(hm, pv[rows] * inv[rows], o)
        ls = jnp.where(hm, jnp.broadcast_to(lse[rows], (nq, ATT_GROUP_W)), ls)
    return o, ls


def _attn_kernel(q1_ref, k1_ref, v1_ref, k1p_ref, v1p_ref, q2_ref, k2_ref, v2_ref, z3_ref,
                 o1_ref, l1_ref, o2_ref, l2_ref, o3_ref, l3_ref):
    nb = ATT_BLOCK
    gw = ATT_GROUP_W
    step = pl.program_id(1)
    lane = lax.broadcasted_iota(jnp.int32, (1, gw), 1)
    head_masks = [(lane // ATT_HEAD_DIM) == h for h in range(ATT_HEADS)]
    qi = lax.broadcasted_iota(jnp.int32, (nb, 2 * nb), 0)
    kj = lax.broadcasted_iota(jnp.int32, (nb, 2 * nb), 1)
    ok_two = (kj >= qi) & (kj <= qi + nb)
    ok_cur = (kj >= nb) & (kj <= qi + nb)
    bias_two = jnp.where(ok_two, 0.0, NEG_BIG).astype(F32)
    bias_first = jnp.where(ok_cur, 0.0, NEG_BIG).astype(F32)
    bias_one = bias_first[:, nb:]

    def blocks(i):
        return slice(i * nb, (i + 1) * nb), slice((i - 1) * nb, (i + 1) * nb)

    def group0():
        jobs = []
        for i in range(4):
            rows, keys = blocks(i)
            if i == 0:
                kk = jnp.concatenate([k1p_ref[0], k1_ref[0, rows]], axis=0)
                vv = jnp.concatenate([v1p_ref[0], v1_ref[0, rows]], axis=0)
                bias = jnp.where(step == 0, bias_first, bias_two)
            else:
                kk, vv, bias = k1_ref[0, keys], v1_ref[0, keys], bias_two
            jobs.append((q1_ref[0, rows], kk, vv, bias, (o1_ref, l1_ref, rows, slice(None))))
        return jobs

    def group1():
        jobs = []
        for i in range(4):
            rows, keys = blocks(i)
            if i == 0:
                kk, vv, bias = k2_ref[0, rows], v2_ref[0, rows], bias_one
            else:
                kk, vv, bias = k2_ref[0, keys], v2_ref[0, keys], bias_two
            jobs.append((q2_ref[0, rows], kk, vv, bias, (o2_ref, l2_ref, rows, slice(None))))
        return jobs

    def group2():
        jobs = []
        for r in range(4):
            c0 = r * 3 * gw
            jobs.append((z3_ref[0, :, c0:c0 + gw], z3_ref[0, :, c0 + gw:c0 + 2 * gw],
                         z3_ref[0, :, c0 + 2 * gw:c0 + 3 * gw], bias_one,
                         (o3_ref, l3_ref, slice(None), slice(r * gw, (r + 1) * gw))))
        return jobs

    def scores(jobs):
        return [_attn_scores(q, kk, bias, head_masks) for q, kk, _, bias, _ in jobs]

    def finish(jobs, ss):
        probs = [_attn_probs(s) for s in ss]
        pvs = [_dot(p, job[2]) for (p, _, _), job in zip(probs, jobs)]
        for pv, (_, inv, lse), job in zip(pvs, probs, jobs):
            o_ref, l_ref, rows, cols = job[4]
            o, ls = _attn_out(pv, inv, lse, head_masks)
            o_ref[0, rows, cols] = o
            l_ref[0, rows, cols] = ls

    j0, j1, j2 = group0(), group1(), group2()
    s0 = scores(j0)
    s1 = scores(j1)
    finish(j0, s0)
    s2 = scores(j2)
    finish(j1, s1)
    finish(j2, s2)


def _attention(za0, z2, z3):
    bsz, t, _ = za0.shape
    gw, nb = ATT_GROUP_W, ATT_BLOCK
    steps = 4
    rows1 = t // steps
    assert rows1 == 4 * nb and t // DILATIONS[1] == 4 * nb and t // DILATIONS[2] == nb
    prev = lambda b, j: (b, jnp.maximum(4 * j - 1, 0), 1)
    prev_v = lambda b, j: (b, jnp.maximum(4 * j - 1, 0), 2)
    in_specs = [
        pl.BlockSpec((1, rows1, gw), lambda b, j: (b, j, 0)),
        pl.BlockSpec((1, rows1, gw), lambda b, j: (b, j, 1)),
        pl.BlockSpec((1, rows1, gw), lambda b, j: (b, j, 2)),
        pl.BlockSpec((1, nb, gw), prev),
        pl.BlockSpec((1, nb, gw), prev_v),
        pl.BlockSpec((1, 4 * nb, gw), lambda b, j: (b, 0, 3 * j)),
        pl.BlockSpec((1, 4 * nb, gw), lambda b, j: (b, 0, 3 * j + 1)),
        pl.BlockSpec((1, 4 * nb, gw), lambda b, j: (b, 0, 3 * j + 2)),
        pl.BlockSpec((1, nb, 4 * 3 * gw), lambda b, j: (b, 0, j)),
    ]
    o1s = pl.BlockSpec((1, rows1, gw), lambda b, j: (b, j, 0))
    o2s = pl.BlockSpec((1, 4 * nb, gw), lambda b, j: (b, 0, j))
    o3s = pl.BlockSpec((1, nb, 4 * gw), lambda b, j: (b, 0, j))
    s1 = jax.ShapeDtypeStruct((bsz, t, gw), F32)
    s2 = jax.ShapeDtypeStruct((bsz, t // DILATIONS[1], DILATIONS[1] * gw), F32)
    s3 = jax.ShapeDtypeStruct((bsz, t // DILATIONS[2], DILATIONS[2] * gw), F32)
    o1, l1, o2, l2, o3, l3 = pl.pallas_call(
        _attn_kernel,
        grid=(bsz, steps),
        in_specs=in_specs,
        out_specs=[o1s, o1s, o2s, o2s, o3s, o3s],
        out_shape=[s1, s1, s2, s2, s3, s3],
        compiler_params=pltpu.CompilerParams(
            dimension_semantics=("arbitrary", "arbitrary"), vmem_limit_bytes=VMEM_LIMIT),
        name="dilated_attn",
    )(za0, za0, za0, za0, za0, z2, z2, z2, z3)
    return o1, l1, o2, l2, o3, l3


def _softplus(x):
    return jnp.maximum(x, 0.0) + jnp.log(1.0 + jnp.exp(-jnp.abs(x)))


def _chunk_masks():
    c = DN_CHUNK
    ii = lax.broadcasted_iota(jnp.int32, (c, c), 0)
    jj = lax.broadcasted_iota(jnp.int32, (c, c), 1)
    return ii >= jj, ii > jj, ii <= jj


def _chunk_decay(ab, abt, alog_ref, dtb_ref, alog_c_ref, dtb_c_ref, tril, triu):
    g_col = -jnp.exp(alog_ref[...]) * _softplus(ab + dtb_ref[...])
    g_row = -jnp.exp(alog_c_ref[...]) * _softplus(abt + dtb_c_ref[...])
    gc_col = _dot_exact(jnp.where(tril, 1.0, 0.0).astype(F32), g_col)
    gc_row = _dot_exact(g_row, jnp.where(triu, 1.0, 0.0).astype(F32))
    return gc_col, gc_row, _sigmoid(ab)


def _head_decay(gc_col, gc_row, beta_col, h, tril):
    c, hd = DN_CHUNK, DN_HEAD_DIM
    cb = jnp.broadcast_to(gc_col[:, h:h + 1], (c, hd))
    bb = jnp.broadcast_to(beta_col[:, DN_HEADS + h:DN_HEADS + h + 1], (c, hd))
    rb = jnp.broadcast_to(gc_row[h:h + 1, :], (c, c))
    ld = jnp.exp(jnp.where(tril, cb[:, :c] - rb, 0.0))
    return cb, bb, ld


def _dn_prep_kernel(k_ref, ab_ref, abt_ref, alog_ref, dtb_ref, alog_c_ref, dtb_c_ref, a_ref,
                    gcc_ref, gcr_ref, *, n_chunks):
    c, hd = DN_CHUNK, DN_HEAD_DIM
    tril, strict, triu = _chunk_masks()
    lane16 = lax.broadcasted_iota(jnp.int32, (c, 2 * DN_HEADS), 1)
    for n in range(n_chunks):
        r0 = n * c
        gc_col, gc_row, beta_col = _chunk_decay(ab_ref[0, r0:r0 + c, :], abt_ref[0, n], alog_ref,
                                                dtb_ref, alog_c_ref, dtb_c_ref, tril, triu)
        gcc_ref[0, r0:r0 + c, :] = jnp.where(lane16 < DN_HEADS, gc_col, beta_col)
        gcr_ref[0, n] = gc_row
        for h in range(DN_HEADS):
            k = k_ref[0, r0:r0 + c, h * hd:(h + 1) * hd]
            _, bb, ld = _head_decay(gc_col, gc_row, beta_col, h, tril)
            a_ref[0, n, h] = jnp.where(strict, _dot_nt(k, k) * (bb[:, :c] * ld), 0.0)


def _dn_solve_kernel(a_ref, t_ref, at_scr, tt_scr):
    c = DN_CHUNK
    tiles = c * c // LANE
    per = LANE // c
    for t in range(tiles):
        blk = a_ref[:, t * LANE:(t + 1) * LANE].T
        for r in range(per):
            at_scr[t * per + r] = blk[r * c:(r + 1) * c]
    row_id = lax.broadcasted_iota(jnp.int32, (SUBLANE, LANE), 0)
    zero = jnp.zeros((SUBLANE, LANE), F32)
    for i in range(c):
        groups = i // SUBLANE + 1
        acc = [zero] * (c // SUBLANE)
        acc[groups - 1] = jnp.where(row_id == i % SUBLANE, 1.0, 0.0).astype(F32)
        for j in range(i):
            a = jnp.broadcast_to(at_scr[i, j:j + 1, :], (SUBLANE, LANE))
            for g in range(j // SUBLANE + 1):
                acc[g] = acc[g] - a * tt_scr[j, g * SUBLANE:(g + 1) * SUBLANE, :]
        for g in range(c // SUBLANE):
            tt_scr[i, g * SUBLANE:(g + 1) * SUBLANE, :] = acc[g]
    for t in range(tiles):
        blk = jnp.concatenate([tt_scr[t * per + r] for r in range(per)], axis=0)
        t_ref[:, t * LANE:(t + 1) * LANE] = blk.T.astype(t_ref.dtype)


def _dn_scan_kernel(q_ref, k_ref, v_ref, gcc_ref, gcr_ref, t_ref, o_ref, s_ref, *, n_chunks):
    c, hd = DN_CHUNK, DN_HEAD_DIM

    @pl.when(pl.program_id(1) == 0)
    def _():
        s_ref[...] = jnp.zeros(s_ref.shape, F32)

    tril, _, _ = _chunk_masks()
    heads = range(DN_HEADS)

    def stage_a(n):
        r0 = n * c
        gc_col = beta_col = gcc_ref[0, r0:r0 + c, :]
        gc_row = gcr_ref[0, n]
        dec, qk, xs, egc = [], [], [], []
        for h in heads:
            dec.append(_head_decay(gc_col, gc_row, beta_col, h, tril))
        for h in heads:
            lanes = slice(h * hd, (h + 1) * hd)
            qk.append(_dot_nt(q_ref[0, r0:r0 + c, lanes], k_ref[0, r0:r0 + c, lanes]))
        for h in heads:
            lanes = slice(h * hd, (h + 1) * hd)
            cb, bb, _ = dec[h]
            eb = jnp.exp(cb)
            egc.append(eb.astype(BF16))
            rhs = jnp.concatenate([v_ref[0, r0:r0 + c, lanes] * bb.astype(BF16),
                                   k_ref[0, r0:r0 + c, lanes] * (bb * eb).astype(BF16)], axis=1)
            xs.append(_dot(t_ref[0, n, h], rhs))
        per_head = []
        for h in heads:
            lanes = slice(h * hd, (h + 1) * hd)
            cb, _, ld = dec[h]
            k = k_ref[0, r0:r0 + c, lanes].astype(F32)
            gl = cb[c - 1:c, :]
            attn = jnp.where(tril, qk[h] * ld, 0.0).astype(BF16)
            kdec_t = (k * jnp.exp(gl - cb)).T.astype(BF16)
            lhs_s = jnp.concatenate([xs[h][:, hd:].astype(BF16),
                                     q_ref[0, r0:r0 + c, lanes] * egc[h]], axis=0)
            per_head.append((xs[h][:, :hd], lhs_s, jnp.concatenate([attn, kdec_t], axis=0),
                             jnp.exp(gl)))
        return per_head

    state = [s_ref[h] for h in heads]
    cur = stage_a(0)
    for n in range(n_chunks):
        r0 = n * c
        ws = [_dot(cur[h][1], state[h].astype(BF16)) for h in heads]
        nxt = stage_a(n + 1) if n + 1 < n_chunks else None
        vn = [(cur[h][0] - ws[h][:c]).astype(BF16) for h in heads]
        upd = [_dot(cur[h][2], vn[h]) for h in heads]
        for h in heads:
            lanes = slice(h * hd, (h + 1) * hd)
            o_ref[0, r0:r0 + c, lanes] = (ws[h][c:] + upd[h][:c]).astype(BF16)
            state[h] = state[h] * cur[h][3] + upd[h][c:]
        cur = nxt
    for h in heads:
        s_ref[h] = state[h]


def _deltanet(dq, dk, dv, ab, abt, a_log, dt_bias, rows=256, scan_rows=1024):
    bsz, t, w = dq.shape
    c = DN_CHUNK
    n_chunks = rows // c
    total_chunks = t // c
    abt_c = abt.reshape(bsz, 2 * DN_HEADS, total_chunks, c).transpose(0, 2, 1, 3)
    pad = jnp.zeros((DN_HEADS,), F32)
    alog16 = jnp.concatenate([a_log, pad])
    dtb16 = jnp.concatenate([dt_bias, pad])
    small = [alog16.reshape(1, -1), dtb16.reshape(1, -1), alog16.reshape(-1, 1), dtb16.reshape(-1, 1)]
    small_specs = [_const_spec((1, 2 * DN_HEADS), single_buffer=False)] * 2 + \
                  [_const_spec((2 * DN_HEADS, 1), single_buffer=False)] * 2
    row = pl.BlockSpec((1, rows, w), lambda b, i: (b, i, 0))
    ab_spec = pl.BlockSpec((1, rows, 2 * DN_HEADS), lambda b, i: (b, i, 0))
    abt_spec = pl.BlockSpec((1, n_chunks, 2 * DN_HEADS, c), lambda b, i: (b, i, 0, 0))
    mat_spec = pl.BlockSpec((1, n_chunks, DN_HEADS, c, c), lambda b, i: (b, i, 0, 0, 0))
    mat_shape = jax.ShapeDtypeStruct((bsz, total_chunks, DN_HEADS, c, c), F32)
    params = pltpu.CompilerParams(dimension_semantics=("arbitrary", "arbitrary"),
                                  vmem_limit_bytes=VMEM_LIMIT)

    a_mat, gcc, gcr = pl.pallas_call(
        functools.partial(_dn_prep_kernel, n_chunks=n_chunks),
        grid=(bsz, t // rows),
        in_specs=[row, ab_spec, abt_spec] + small_specs,
        out_specs=[mat_spec, ab_spec, abt_spec],
        out_shape=[mat_shape, jax.ShapeDtypeStruct(ab.shape, F32),
                   jax.ShapeDtypeStruct(abt_c.shape, F32)],
        compiler_params=params,
        name="dn_prep",
    )(dk, ab, abt_c, *small)

    n_sys = bsz * total_chunks * DN_HEADS
    assert n_sys % LANE == 0
    flat_spec = pl.BlockSpec((LANE, c * c), lambda i: (i, 0))
    t_mat = pl.pallas_call(
        _dn_solve_kernel,
        grid=(n_sys // LANE,),
        in_specs=[flat_spec],
        out_specs=flat_spec,
        out_shape=jax.ShapeDtypeStruct((n_sys, c * c), BF16),
        scratch_shapes=[pltpu.VMEM((c, c, LANE), F32), pltpu.VMEM((c, c, LANE), F32)],
        compiler_params=pltpu.CompilerParams(dimension_semantics=("arbitrary",),
                                             vmem_limit_bytes=VMEM_LIMIT),
        name="dn_solve",
    )(a_mat.reshape(n_sys, c * c)).reshape(mat_shape.shape)

    sn = scan_rows // c
    srow = pl.BlockSpec((1, scan_rows, w), lambda b, i: (b, i, 0))
    sab_spec = pl.BlockSpec((1, scan_rows, 2 * DN_HEADS), lambda b, i: (b, i, 0))
    sabt_spec = pl.BlockSpec((1, sn, 2 * DN_HEADS, c), lambda b, i: (b, i, 0, 0))
    smat_spec = pl.BlockSpec((1, sn, DN_HEADS, c, c), lambda b, i: (b, i, 0, 0, 0))
    return pl.pallas_call(
        functools.partial(_dn_scan_kernel, n_chunks=sn),
        grid=(bsz, t // scan_rows),
        in_specs=[srow, srow, srow, sab_spec, sabt_spec, smat_spec],
        out_specs=srow,
        out_shape=jax.ShapeDtypeStruct((bsz, t, w), BF16),
        scratch_shapes=[pltpu.VMEM((DN_HEADS, DN_HEAD_DIM, DN_HEAD_DIM), F32)],
        compiler_params=params,
        name="dn_scan",
    )(dq, dk, dv, gcc, gcr, t_mat)


def _merge_kernel(o1_ref, l1_ref, o2_ref, l2_ref, o3_ref, l3_ref, odn_ref, gate_ref, nrm_ref,
                  mg_ref, x_ref, gt_ref, wpa_ref, wpd_ref, wo_ref, out_ref, nat_scr):
    d = D_MODEL
    tm = x_ref.shape[1]
    slabs = ATT_GROUP_W // LANE
    for a, (ref, dil) in enumerate(((o2_ref, DILATIONS[1]), (l2_ref, DILATIONS[1]),
                                    (o3_ref, DILATIONS[2]), (l3_ref, DILATIONS[2]))):
        for r in range(dil):
            for s in range(slabs):
                lo = r * ATT_GROUP_W + s * LANE
                nat_scr[a, s, pl.ds(r, tm // dil, stride=dil), :] = ref[0, :, lo:lo + LANE]
    halves = [slice(i * (tm // 2), (i + 1) * (tm // 2)) for i in range(2)]
    yd = []
    for r in halves:
        gated = []
        for h in range(DN_HEADS):
            lanes = slice(h * DN_HEAD_DIM, (h + 1) * DN_HEAD_DIM)
            o = odn_ref[0, r, lanes].astype(F32)
            ms = jnp.mean(o * o, axis=-1, keepdims=True)
            on = (o * lax.rsqrt(ms + EPS)) * nrm_ref[...]
            gated.append((on * _silu(gate_ref[0, r, lanes].astype(F32))).astype(BF16))
        yd.append(_dot(jnp.concatenate(gated, axis=1), wpd_ref[...]))
    ya = []
    for r in halves:
        ys = []
        for s in range(slabs):
            cols = slice(s * LANE, (s + 1) * LANE)
            l1, l2, l3 = l1_ref[0, r, cols], nat_scr[1, s, r, :], nat_scr[3, s, r, :]
            m = jnp.maximum(jnp.maximum(l1, l2), l3)
            e1, e2, e3 = jnp.exp(l1 - m), jnp.exp(l2 - m), jnp.exp(l3 - m)
            y = e1 * o1_ref[0, r, cols] + e2 * nat_scr[0, s, r, :] + e3 * nat_scr[2, s, r, :]
            ys.append((y / (e1 + e2 + e3)).astype(BF16))
        ya.append(_dot(jnp.concatenate(ys, axis=1), wpa_ref[...]))
    for i, r in enumerate(halves):
        mix = mg_ref[0, r, :d].astype(F32) * ya[i] + mg_ref[0, r, d:].astype(F32) * yd[i]
        out = _dot(mix.astype(BF16), wo_ref[...])
        out_ref[0, r] = x_ref[0, r] + gt_ref[0] * out


def _merge(att, odn, gate, dn_norm, mg, x, mod_l, wpa, wpd, wo, layer, tm=512):
    bsz, t, d = x.shape
    row = lambda n: pl.BlockSpec((1, tm, n), lambda b, i: (b, i, 0))
    view = lambda dil: pl.BlockSpec((1, tm // dil, dil * ATT_GROUP_W), lambda b, i: (b, i, 0))
    att_specs = [row(ATT_GROUP_W)] * 2 + [view(DILATIONS[1])] * 2 + [view(DILATIONS[2])] * 2
    return pl.pallas_call(
        _merge_kernel,
        grid=(bsz, t // tm),
        in_specs=att_specs + [row(DN_W), row(DN_W), _const_spec((1, DN_HEAD_DIM), single_buffer=False),
                              row(2 * d), row(d), _mod_spec(5),
                              _layer_spec(wpa.shape[1:], layer), _layer_spec(wpd.shape[1:], layer),
                              _layer_spec(wo.shape[1:], layer)],
        out_specs=row(d),
        out_shape=jax.ShapeDtypeStruct(x.shape, F32),
        scratch_shapes=[pltpu.VMEM((4, ATT_GROUP_W // LANE, tm, LANE), F32)],
        compiler_params=pltpu.CompilerParams(
            dimension_semantics=("arbitrary", "arbitrary"), vmem_limit_bytes=VMEM_LIMIT),
        name="merge_out",
    )(*att, odn, gate, dn_norm.reshape(1, -1), mg, x, mod_l, wpa, wpd, wo)


def _mixer(x, mod_l, layer, gain, w_packed, ones_blk, qn4, kn4, conv_w, a_log, dt_bias, dn_norm,
           wpa, wpd, wo):
    za0, za1, za2, dq, dk, dv, gate, mg, ab, abt = _inproj(
        x, mod_l, gain, w_packed, layer, ones_blk, qn4, kn4, conv_w)
    att = _attention(za0, za1, za2)
    odn = _deltanet(dq, dk, dv, ab, abt, a_log, dt_bias)
    return _merge(att, odn, gate, dn_norm, mg, x, mod_l, wpa, wpd, wo, layer)


def kernel(x, c, ada_w, ada_b, norm_ff1, ffn1_w_up, ffn1_w_down, norm_mix, w_in, q_norm, k_norm,
           conv_w, a_log, dt_bias, dn_norm, w_proj_att, w_proj_dn, w_out, norm_ff2, ffn2_w_up,
           ffn2_w_down):
    bsz = x.shape[0]
    mod = _ada_mod(c, ada_w, ada_b)
    blk = jnp.arange(ATT_GROUP_W) // ATT_HEAD_DIM
    ones_blk = jnp.where(blk[:, None] == blk[None, :], 1.0 / ATT_HEAD_DIM, 0.0).astype(BF16)
    w_in_packed = _pack_w_in(w_in)
    up1, dn1, up2, dn2 = (w.astype(BF16) for w in (ffn1_w_up, ffn1_w_down, ffn2_w_up, ffn2_w_down))
    wpa, wpd, wo = (w.astype(BF16) for w in (w_proj_att, w_proj_dn, w_out))
    for l in range(DEPTH):
        mod_l = mod[l].reshape(bsz, 1, N_ADA * D_MODEL)
        x = _ffn(x, mod_l, 0, norm_ff1[l], up1, dn1, l)
        x = _mixer(x, mod_l, l, norm_mix[l], w_in_packed, ones_blk,
                   jnp.tile(q_norm[l], ATT_HEADS).reshape(1, -1),
                   jnp.tile(k_norm[l], ATT_HEADS).reshape(1, -1),
                   conv_w[l], a_log[l], dt_bias[l], dn_norm[l], wpa, wpd, wo)
        x = _ffn(x, mod_l, 6, norm_ff2[l], up2, dn2, l)
    return x
```

```python
import functools

import jax
import jax.numpy as jnp
from jax import lax
from jax.experimental import pallas as pl
from jax.experimental.pallas import tpu as pltpu

F32 = jnp.float32
BF16 = jnp.bfloat16

D_MODEL = 1024
DEPTH = 4
SEQ = 2048
D_FF = 2816
EPS = 1e-6
N_ADA = 9

ATT_HEAD_DIM = 64
ATT_HEADS = 4
ATT_GROUP_W = ATT_HEADS * ATT_HEAD_DIM
DILATIONS = (1, 4, 16)
ATT_BLOCK = 128
N_GROUPS = 3
ATT_W = N_GROUPS * ATT_GROUP_W

DN_HEADS = 8
DN_HEAD_DIM = 128
DN_W = DN_HEADS * DN_HEAD_DIM
DN_CHUNK = 64
CONV_W = 4

OFF_DN_QKV = 3 * ATT_W
OFF_DN_GATE = OFF_DN_QKV + 3 * DN_W
OFF_DN_A = OFF_DN_GATE + DN_W
OFF_DN_B = OFF_DN_A + DN_HEADS
OFF_MERGE = OFF_DN_B + DN_HEADS
N_IN = OFF_MERGE + 2 * D_MODEL

P_DN = 0
P_GATE = P_DN + 3 * DN_W
P_MERGE = P_GATE + DN_W
P_ATT = P_MERGE + 2 * D_MODEL
P_AB = P_ATT + 3 * ATT_W
P_TOTAL = P_AB + 256

LANE = 128
SUBLANE = 8
NEG_BIG = -1e30
VMEM_LIMIT = 56 * 1024 * 1024
INPROJ_LOOKAHEAD = 1


def _dot(a, b):
    return jnp.dot(a, b, preferred_element_type=F32)


def _dot_nt(a, b):
    return lax.dot_general(a, b, (((1,), (1,)), ((), ())), preferred_element_type=F32)


def _dot_exact(a, b):
    return jnp.dot(a, b, preferred_element_type=F32, precision=lax.Precision.HIGHEST)


def _sigmoid(x):
    return 1.0 / (1.0 + jnp.exp(-x))


def _silu(x):
    return x * _sigmoid(x)


def _norm_mod(x, gain, shift, scale):
    ms = jnp.mean(x * x, axis=-1, keepdims=True)
    y = (x * lax.rsqrt(ms + EPS)) * gain
    return y * (1.0 + scale) + shift


def _const_spec(shape, single_buffer=True):
    nd = len(shape)
    kwargs = {"pipeline_mode": pl.Buffered(1)} if single_buffer else {}
    return pl.BlockSpec(shape, lambda *_: (0,) * nd, **kwargs)


def _layer_spec(shape, layer):
    nd = len(shape)
    return pl.BlockSpec((None,) + tuple(shape), lambda *_: (layer,) + (0,) * nd,
                        pipeline_mode=pl.Buffered(1))


def _mod_spec(k):
    return pl.BlockSpec((1, 1, D_MODEL), lambda b, i: (b, 0, k))


def _ada_kernel(c_ref, w_ref, b_ref, o_ref):
    c = c_ref[...]
    ca = _silu(c).astype(BF16)
    o_ref[0] = _dot(ca, w_ref[0].astype(BF16)) + b_ref[0]


def _ada_mod(c, ada_w, ada_b, tn=1024):
    depth, d, n = ada_w.shape
    bsz = c.shape[0]
    return pl.pallas_call(
        _ada_kernel,
        grid=(depth, n // tn),
        in_specs=[
            pl.BlockSpec((bsz, d), lambda l, j: (0, 0)),
            pl.BlockSpec((1, d, tn), lambda l, j: (l, 0, j)),
            pl.BlockSpec((1, 1, tn), lambda l, j: (l, 0, j)),
        ],
        out_specs=pl.BlockSpec((1, bsz, tn), lambda l, j: (l, 0, j)),
        out_shape=jax.ShapeDtypeStruct((depth, bsz, n), F32),
        name="ada_mod",
    )(c, ada_w, ada_b.reshape(depth, 1, n))


def _ffn_kernel(x_ref, g_ref, sh_ref, sc_ref, gt_ref, wup_ref, wdn_ref, o_ref, *, tf):
    x = x_ref[0]
    hb = _norm_mod(x, g_ref[...], sh_ref[0], sc_ref[0]).astype(BF16)
    acc = jnp.zeros(x.shape, F32)
    for f in range(D_FF // tf):
        gate = _dot(hb, wup_ref[:, f * tf:(f + 1) * tf])
        up = _dot(hb, wup_ref[:, D_FF + f * tf:D_FF + (f + 1) * tf])
        a = (_silu(gate) * up).astype(BF16)
        acc = acc + _dot(a, wdn_ref[f * tf:(f + 1) * tf, :])
    o_ref[0] = x + (0.5 * gt_ref[0]) * acc


def _ffn(x, mod_l, k0, gain, wup, wdn, layer, tm=512, tf=256):
    bsz, t, d = x.shape
    return pl.pallas_call(
        functools.partial(_ffn_kernel, tf=tf),
        grid=(bsz, t // tm),
        in_specs=[
            pl.BlockSpec((1, tm, d), lambda b, i: (b, i, 0)),
            _const_spec((1, d), single_buffer=False),
            _mod_spec(k0), _mod_spec(k0 + 1), _mod_spec(k0 + 2),
            _layer_spec((d, 2 * D_FF), layer),
            _layer_spec((D_FF, d), layer),
        ],
        out_specs=pl.BlockSpec((1, tm, d), lambda b, i: (b, i, 0)),
        out_shape=jax.ShapeDtypeStruct(x.shape, F32),
        compiler_params=pltpu.CompilerParams(
            dimension_semantics=("arbitrary", "arbitrary"), vmem_limit_bytes=VMEM_LIMIT),
        name="ffn",
    )(x, gain.reshape(1, d), mod_l, mod_l, mod_l, wup, wdn)


PACK_GROUP = 256
PACK_TILE = 2 * PACK_GROUP


def _pack_starts():
    gw = ATT_GROUP_W
    starts = list(range(OFF_DN_QKV, OFF_DN_A, PACK_GROUP))
    starts += list(range(OFF_MERGE, N_IN, PACK_GROUP))
    starts += [part * ATT_W + g * gw for g in range(N_GROUPS) for part in range(3)]
    starts.append(OFF_DN_A)
    assert len(starts) * PACK_GROUP == P_TOTAL
    return starts


def _pack_w_in_kernel(starts_ref, wa_ref, wb_ref, o_ref):
    del starts_ref
    o_ref[0, :, :PACK_GROUP] = wa_ref[0].T.astype(BF16)
    last = pl.program_id(1) == pl.num_programs(1) - 1
    keep = jnp.where(last, 2 * DN_HEADS, PACK_GROUP)
    lane = lax.broadcasted_iota(jnp.int32, (1, PACK_GROUP), 1)
    o_ref[0, :, PACK_GROUP:] = jnp.where(lane < keep, wb_ref[0].T, 0.0).astype(BF16)


def _pack_w_in(w):
    depth, d, n = w.shape
    starts = jnp.asarray(_pack_starts(), jnp.int32)
    group = (pl.Element(1), pl.Element(PACK_GROUP), pl.Element(d))
    grid_spec = pltpu.PrefetchScalarGridSpec(
        num_scalar_prefetch=1,
        grid=(depth, P_TOTAL // PACK_TILE),
        in_specs=[pl.BlockSpec(group, lambda l, j, st: (l, pl.multiple_of(st[2 * j], SUBLANE), 0)),
                  pl.BlockSpec(group, lambda l, j, st: (l, pl.multiple_of(st[2 * j + 1], SUBLANE), 0))],
        out_specs=pl.BlockSpec((1, d, PACK_TILE), lambda l, j, st: (l, 0, j)),
    )
    wt = jnp.swapaxes(w, 1, 2)
    return pl.pallas_call(
        _pack_w_in_kernel,
        grid_spec=grid_spec,
        out_shape=jax.ShapeDtypeStruct((depth, d, P_TOTAL), BF16),
        compiler_params=pltpu.CompilerParams(
            dimension_semantics=("arbitrary", "arbitrary"), vmem_limit_bytes=VMEM_LIMIT),
        name="pack_w_in",
    )(starts, wt, wt)


def _inproj_kernel(x_ref, g_ref, sh_ref, sc_ref, w_ref, ones_ref, qn_ref, kn_ref, cw_ref,
                   za0_ref, za1_ref, za2_ref, dq_ref, dk_ref, dv_ref, gate_ref, mg_ref,
                   ab_ref, abt_ref, carry_ref, dscr_ref, *, tm, cn):
    x = x_ref[0]
    hb = _norm_mod(x, g_ref[...], sh_ref[0], sc_ref[0]).astype(BF16)

    @pl.when(pl.program_id(1) == 0)
    def _():
        carry_ref[...] = jnp.zeros(carry_ref.shape, F32)

    jobs = []
    gw = ATT_GROUP_W

    def att_store(za_ref, g, part, val):
        dil = DILATIONS[g]
        if dil == 1:
            za_ref[0, :, part * gw:(part + 1) * gw] = val.astype(BF16)
            return
        for s in range(gw // LANE):
            dscr_ref[s] = val[:, s * LANE:(s + 1) * LANE]
        for r in range(dil):
            for s in range(gw // LANE):
                lo = r * 3 * gw + part * gw + s * LANE
                za_ref[0, :, lo:lo + LANE] = dscr_ref[s, pl.ds(r, tm // dil, stride=dil), :].astype(BF16)

    def att_norm(za_ref, g, part, gain_ref, mult):
        def epilogue(z):
            ms = _dot((z * z).astype(BF16), ones_ref[...])
            att_store(za_ref, g, part, (z * lax.rsqrt(ms + EPS)) * (gain_ref[...] * mult))
        return epilogue

    def att_plain(za_ref, g):
        def epilogue(z):
            att_store(za_ref, g, 2, z)
        return epilogue

    light = []
    for g, za_ref in enumerate((za0_ref, za1_ref, za2_ref)):
        c0 = P_ATT + g * 3 * gw
        light.append((c0, gw, att_norm(za_ref, g, 0, qn_ref, ATT_HEAD_DIM ** -0.5)))
        light.append((c0 + gw, gw, att_norm(za_ref, g, 1, kn_ref, 1.0)))
        light.append((c0 + 2 * gw, gw, att_plain(za_ref, g)))

    def dn_conv(out_ref, part, j):
        dc = part * DN_W + j * cn

        def epilogue(z):
            xe = jnp.concatenate([carry_ref[:, dc:dc + cn], z], axis=0)
            carry_ref[:, dc:dc + cn] = z[tm - SUBLANE:tm, :]
            y = cw_ref[CONV_W - 1:CONV_W, dc:dc + cn] * z
            for s in range(1, CONV_W):
                shifted = pltpu.roll(xe, s, axis=0)[SUBLANE:, :]
                y = y + cw_ref[CONV_W - 1 - s:CONV_W - s, dc:dc + cn] * shifted
            y = _silu(y)
            if part < 2:
                mult = DN_HEAD_DIM ** -0.5 if part == 0 else 1.0
                for h in range(cn // DN_HEAD_DIM):
                    yh = y[:, h * DN_HEAD_DIM:(h + 1) * DN_HEAD_DIM]
                    ss = jnp.sum(yh * yh, axis=-1, keepdims=True)
                    yn = yh * (lax.rsqrt(ss + EPS) * mult)
                    lo = j * cn + h * DN_HEAD_DIM
                    out_ref[0, :, lo:lo + DN_HEAD_DIM] = yn.astype(BF16)
            else:
                out_ref[0, :, j * cn:(j + 1) * cn] = y.astype(BF16)
        return epilogue

    heavy = []
    for part, out_ref in enumerate((dq_ref, dk_ref, dv_ref)):
        for j in range(DN_W // cn):
            heavy.append((P_DN + part * DN_W + j * cn, cn, dn_conv(out_ref, part, j)))

    def store_cast(out_ref, j, fn):
        def epilogue(z):
            out_ref[0, :, j * cn:(j + 1) * cn] = fn(z).astype(BF16)
        return epilogue

    for j in range(DN_W // cn):
        light.append((P_GATE + j * cn, cn, store_cast(gate_ref, j, lambda z: z)))
    for j in range(2 * D_MODEL // cn):
        light.append((P_MERGE + j * cn, cn, store_cast(mg_ref, j, _sigmoid)))

    def ab_epilogue(z):
        ab_ref[0] = z[:, :2 * DN_HEADS]
        abt_ref[0] = z.T[:2 * DN_HEADS, :]

    light.append((P_AB, 128, ab_epilogue))

    per_heavy = -(-len(light) // len(heavy))
    for i, job in enumerate(heavy):
        jobs.append(job)
        jobs.extend(light[i * per_heavy:(i + 1) * per_heavy])

    def project(job):
        return _dot(hb, w_ref[:, job[0]:job[0] + job[1]])

    pending = [project(job) for job in jobs[:INPROJ_LOOKAHEAD]]
    for i, job in enumerate(jobs):
        if i + INPROJ_LOOKAHEAD < len(jobs):
            pending.append(project(jobs[i + INPROJ_LOOKAHEAD]))
        job[2](pending.pop(0))


def _inproj(x, mod_l, gain, w_packed, layer, ones_blk, qn4, kn4, conv_w, tm=512, cn=512):
    bsz, t, d = x.shape
    bf = lambda n: jax.ShapeDtypeStruct((bsz, t, n), BF16)
    row = lambda n: pl.BlockSpec((1, tm, n), lambda b, i: (b, i, 0))
    att_spec = lambda dil: pl.BlockSpec((1, tm // dil, dil * 3 * ATT_GROUP_W), lambda b, i: (b, i, 0))
    att_shape = lambda dil: jax.ShapeDtypeStruct((bsz, t // dil, dil * 3 * ATT_GROUP_W), BF16)
    return pl.pallas_call(
        functools.partial(_inproj_kernel, tm=tm, cn=cn),
        grid=(bsz, t // tm),
        in_specs=[
            row(d),
            _const_spec((1, d), single_buffer=False),
            _mod_spec(3), _mod_spec(4),
            _layer_spec((d, P_TOTAL), layer),
            _const_spec((ATT_GROUP_W, ATT_GROUP_W)),
            _const_spec((1, ATT_GROUP_W), single_buffer=False),
            _const_spec((1, ATT_GROUP_W), single_buffer=False),
            _const_spec((CONV_W, 3 * DN_W), single_buffer=False),
        ],
        out_specs=[att_spec(dil) for dil in DILATIONS] + [row(DN_W)] * 4 + [row(2 * D_MODEL)]
        + [row(2 * DN_HEADS), pl.BlockSpec((1, 2 * DN_HEADS, tm), lambda b, i: (b, 0, i))],
        out_shape=[att_shape(dil) for dil in DILATIONS] + [bf(DN_W)] * 4 + [bf(2 * D_MODEL)]
        + [jax.ShapeDtypeStruct((bsz, t, 2 * DN_HEADS), F32),
           jax.ShapeDtypeStruct((bsz, 2 * DN_HEADS, t), F32)],
        scratch_shapes=[pltpu.VMEM((SUBLANE, 3 * DN_W), F32),
                        pltpu.VMEM((ATT_GROUP_W // LANE, tm, LANE), F32)],
        compiler_params=pltpu.CompilerParams(
            dimension_semantics=("arbitrary", "arbitrary"), vmem_limit_bytes=VMEM_LIMIT),
        name="inproj",
    )(x, gain.reshape(1, d), mod_l, mod_l, w_packed, ones_blk, qn4, kn4, conv_w)


def _attn_scores(q, kk, bias, head_masks):
    zero = jnp.zeros_like(q)
    qs = jnp.concatenate([jnp.where(hm, q, zero) for hm in head_masks], axis=0)
    return _dot_nt(qs, kk) + jnp.concatenate([bias] * ATT_HEADS, axis=0)


def _attn_probs(s):
    m = jnp.max(s, axis=-1, keepdims=True)
    p = jnp.exp(s - m)
    l = jnp.sum(p, axis=-1, keepdims=True)
    return p.astype(BF16), 1.0 / l, m + jnp.log(l)


def _attn_out(pv, inv, lse, head_masks):
    nq = pv.shape[0] // ATT_HEADS
    o = jnp.zeros((nq, ATT_GROUP_W), F32)
    ls = jnp.zeros((nq, ATT_GROUP_W), F32)
    for h, hm in enumerate(head_masks):
        rows = slice(h * nq, (h + 1) * nq)
        o = jnp.where(hm, pv[rows] * inv[rows], o)
        ls = jnp.where(hm, jnp.broadcast_to(lse[rows], (nq, ATT_GROUP_W)), ls)
    return o, ls


def _attn_kernel(q1_ref, k1_ref, v1_ref, k1p_ref, v1p_ref, q2_ref, k2_ref, v2_ref, z3_ref,
                 o1_ref, l1_ref, o2_ref, l2_ref, o3_ref, l3_ref):
    nb = ATT_BLOCK
    gw = ATT_GROUP_W
    step = pl.program_id(1)
    lane = lax.broadcasted_iota(jnp.int32, (1, gw), 1)
    head_masks = [(lane // ATT_HEAD_DIM) == h for h in range(ATT_HEADS)]
    qi = lax.broadcasted_iota(jnp.int32, (nb, 2 * nb), 0)
    kj = lax.broadcasted_iota(jnp.int32, (nb, 2 * nb), 1)
    ok_two = (kj >= qi) & (kj <= qi + nb)
    ok_cur = (kj >= nb) & (kj <= qi + nb)
    bias_two = jnp.where(ok_two, 0.0, NEG_BIG).astype(F32)
    bias_first = jnp.where(ok_cur, 0.0, NEG_BIG).astype(F32)
    bias_one = bias_first[:, nb:]

    def blocks(i):
        return slice(i * nb, (i + 1) * nb), slice((i - 1) * nb, (i + 1) * nb)

    def group0():
        jobs = []
        for i in range(4):
            rows, keys = blocks(i)
            if i == 0:
                kk = jnp.concatenate([k1p_ref[0], k1_ref[0, rows]], axis=0)
                vv = jnp.concatenate([v1p_ref[0], v1_ref[0, rows]], axis=0)
                bias = jnp.where(step == 0, bias_first, bias_two)
            else:
                kk, vv, bias = k1_ref[0, keys], v1_ref[0, keys], bias_two
            jobs.append((q1_ref[0, rows], kk, vv, bias, (o1_ref, l1_ref, rows, slice(None))))
        return jobs

    def group1():
        jobs = []
        for i in range(4):
            rows, keys = blocks(i)
            if i == 0:
                kk, vv, bias = k2_ref[0, rows], v2_ref[0, rows], bias_one
            else:
                kk, vv, bias = k2_ref[0, keys], v2_ref[0, keys], bias_two
            jobs.append((q2_ref[0, rows], kk, vv, bias, (o2_ref, l2_ref, rows, slice(None))))
        return jobs

    def group2():
        jobs = []
        for r in range(4):
            c0 = r * 3 * gw
            jobs.append((z3_ref[0, :, c0:c0 + gw], z3_ref[0, :, c0 + gw:c0 + 2 * gw],
                         z3_ref[0, :, c0 + 2 * gw:c0 + 3 * gw], bias_one,
                         (o3_ref, l3_ref, slice(None), slice(r * gw, (r + 1) * gw))))
        return jobs

    def scores(jobs):
        return [_attn_scores(q, kk, bias, head_masks) for q, kk, _, bias, _ in jobs]

    def finish(jobs, ss):
        probs = [_attn_probs(s) for s in ss]
        pvs = [_dot(p, job[2]) for (p, _, _), job in zip(probs, jobs)]
        for pv, (_, inv, lse), job in zip(pvs, probs, jobs):
            o_ref, l_ref, rows, cols = job[4]
            o, ls = _attn_out(pv, inv, lse, head_masks)
            o_ref[0, rows, cols] = o
            l_ref[0, rows, cols] = ls

    j0, j1, j2 = group0(), group1(), group2()
    s0 = scores(j0)
    s1 = scores(j1)
    finish(j0, s0)
    s2 = scores(j2)
    finish(j1, s1)
    finish(j2, s2)


def _attention(za0, z2, z3):
    bsz, t, _ = za0.shape
    gw, nb = ATT_GROUP_W, ATT_BLOCK
    steps = 4
    rows1 = t // steps
    assert rows1 == 4 * nb and t // DILATIONS[1] == 4 * nb and t // DILATIONS[2] == nb
    prev = lambda b, j: (b, jnp.maximum(4 * j - 1, 0), 1)
    prev_v = lambda b, j: (b, jnp.maximum(4 * j - 1, 0), 2)
    in_specs = [
        pl.BlockSpec((1, rows1, gw), lambda b, j: (b, j, 0)),
        pl.BlockSpec((1, rows1, gw), lambda b, j: (b, j, 1)),
        pl.BlockSpec((1, rows1, gw), lambda b, j: (b, j, 2)),
        pl.BlockSpec((1, nb, gw), prev),
        pl.BlockSpec((1, nb, gw), prev_v),
        pl.BlockSpec((1, 4 * nb, gw), lambda b, j: (b, 0, 3 * j)),
        pl.BlockSpec((1, 4 * nb, gw), lambda b, j: (b, 0, 3 * j + 1)),
        pl.BlockSpec((1, 4 * nb, gw), lambda b, j: (b, 0, 3 * j + 2)),
        pl.BlockSpec((1, nb, 4 * 3 * gw), lambda b, j: (b, 0, j)),
    ]
    o1s = pl.BlockSpec((1, rows1, gw), lambda b, j: (b, j, 0))
    o2s = pl.BlockSpec((1, 4 * nb, gw), lambda b, j: (b, 0, j))
    o3s = pl.BlockSpec((1, nb, 4 * gw), lambda b, j: (b, 0, j))
    s1 = jax.ShapeDtypeStruct((bsz, t, gw), F32)
    s2 = jax.ShapeDtypeStruct((bsz, t // DILATIONS[1], DILATIONS[1] * gw), F32)
    s3 = jax.ShapeDtypeStruct((bsz, t // DILATIONS[2], DILATIONS[2] * gw), F32)
    o1, l1, o2, l2, o3, l3 = pl.pallas_call(
        _attn_kernel,
        grid=(bsz, steps),
        in_specs=in_specs,
        out_specs=[o1s, o1s, o2s, o2s, o3s, o3s],
        out_shape=[s1, s1, s2, s2, s3, s3],
        compiler_params=pltpu.CompilerParams(
            dimension_semantics=("arbitrary", "arbitrary"), vmem_limit_bytes=VMEM_LIMIT),
        name="dilated_attn",
    )(za0, za0, za0, za0, za0, z2, z2, z2, z3)
    return o1, l1, o2, l2, o3, l3


def _softplus(x):
    return jnp.maximum(x, 0.0) + jnp.log(1.0 + jnp.exp(-jnp.abs(x)))


def _chunk_masks():
    c = DN_CHUNK
    ii = lax.broadcasted_iota(jnp.int32, (c, c), 0)
    jj = lax.broadcasted_iota(jnp.int32, (c, c), 1)
    return ii >= jj, ii > jj, ii <= jj


def _chunk_decay(ab, abt, alog_ref, dtb_ref, alog_c_ref, dtb_c_ref, tril, triu):
    g_col = -jnp.exp(alog_ref[...]) * _softplus(ab + dtb_ref[...])
    g_row = -jnp.exp(alog_c_ref[...]) * _softplus(abt + dtb_c_ref[...])
    gc_col = _dot_exact(jnp.where(tril, 1.0, 0.0).astype(F32), g_col)
    gc_row = _dot_exact(g_row, jnp.where(triu, 1.0, 0.0).astype(F32))
    return gc_col, gc_row, _sigmoid(ab)


def _head_decay(gc_col, gc_row, beta_col, h, tril):
    c, hd = DN_CHUNK, DN_HEAD_DIM
    cb = jnp.broadcast_to(gc_col[:, h:h + 1], (c, hd))
    bb = jnp.broadcast_to(beta_col[:, DN_HEADS + h:DN_HEADS + h + 1], (c, hd))
    rb = jnp.broadcast_to(gc_row[h:h + 1, :], (c, c))
    ld = jnp.exp(jnp.where(tril, cb[:, :c] - rb, 0.0))
    return cb, bb, ld


def _dn_prep_kernel(k_ref, ab_ref, abt_ref, alog_ref, dtb_ref, alog_c_ref, dtb_c_ref, a_ref,
                    gcc_ref, gcr_ref, *, n_chunks):
    c, hd = DN_CHUNK, DN_HEAD_DIM
    tril, strict, triu = _chunk_masks()
    lane16 = lax.broadcasted_iota(jnp.int32, (c, 2 * DN_HEADS), 1)
    for n in range(n_chunks):
        r0 = n * c
        gc_col, gc_row, beta_col = _chunk_decay(ab_ref[0, r0:r0 + c, :], abt_ref[0, n], alog_ref,
                                                dtb_ref, alog_c_ref, dtb_c_ref, tril, triu)
        gcc_ref[0, r0:r0 + c, :] = jnp.where(lane16 < DN_HEADS, gc_col, beta_col)
        gcr_ref[0, n] = gc_row
        for h in range(DN_HEADS):
            k = k_ref[0, r0:r0 + c, h * hd:(h + 1) * hd]
            _, bb, ld = _head_decay(gc_col, gc_row, beta_col, h, tril)
            a_ref[0, n, h] = jnp.where(strict, _dot_nt(k, k) * (bb[:, :c] * ld), 0.0)


def _dn_solve_kernel(a_ref, t_ref, at_scr, tt_scr):
    c = DN_CHUNK
    tiles = c * c // LANE
    per = LANE // c
    for t in range(tiles):
        blk = a_ref[:, t * LANE:(t + 1) * LANE].T
        for r in range(per):
            at_scr[t * per + r] = blk[r * c:(r + 1) * c]
    row_id = lax.broadcasted_iota(jnp.int32, (SUBLANE, LANE), 0)
    zero = jnp.zeros((SUBLANE, LANE), F32)
    for i in range(c):
        groups = i // SUBLANE + 1
        acc = [zero] * (c // SUBLANE)
        acc[groups - 1] = jnp.where(row_id == i % SUBLANE, 1.0, 0.0).astype(F32)
        for j in range(i):
            a = jnp.broadcast_to(at_scr[i, j:j + 1, :], (SUBLANE, LANE))
            for g in range(j // SUBLANE + 1):
                acc[g] = acc[g] - a * tt_scr[j, g * SUBLANE:(g + 1) * SUBLANE, :]
        for g in range(c // SUBLANE):
            tt_scr[i, g * SUBLANE:(g + 1) * SUBLANE, :] = acc[g]
    for t in range(tiles):
        blk = jnp.concatenate([tt_scr[t * per + r] for r in range(per)], axis=0)
        t_ref[:, t * LANE:(t + 1) * LANE] = blk.T.astype(t_ref.dtype)


def _dn_scan_kernel(q_ref, k_ref, v_ref, gcc_ref, gcr_ref, t_ref, o_ref, s_ref, *, n_chunks):
    c, hd = DN_CHUNK, DN_HEAD_DIM

    @pl.when(pl.program_id(1) == 0)
    def _():
        s_ref[...] = jnp.zeros(s_ref.shape, F32)

    tril, _, _ = _chunk_masks()
    heads = range(DN_HEADS)

    def stage_a(n):
        r0 = n * c
        gc_col = beta_col = gcc_ref[0, r0:r0 + c, :]
        gc_row = gcr_ref[0, n]
        dec, qk, xs, egc = [], [], [], []
        for h in heads:
            dec.append(_head_decay(gc_col, gc_row, beta_col, h, tril))
        for h in heads:
            lanes = slice(h * hd, (h + 1) * hd)
            qk.append(_dot_nt(q_ref[0, r0:r0 + c, lanes], k_ref[0, r0:r0 + c, lanes]))
        for h in heads:
            lanes = slice(h * hd, (h + 1) * hd)
            cb, bb, _ = dec[h]
            eb = jnp.exp(cb)
            egc.append(eb.astype(BF16))
            rhs = jnp.concatenate([v_ref[0, r0:r0 + c, lanes] * bb.astype(BF16),
                                   k_ref[0, r0:r0 + c, lanes] * (bb * eb).astype(BF16)], axis=1)
            xs.append(_dot(t_ref[0, n, h], rhs))
        per_head = []
        for h in heads:
            lanes = slice(h * hd, (h + 1) * hd)
            cb, _, ld = dec[h]
            k = k_ref[0, r0:r0 + c, lanes].astype(F32)
            gl = cb[c - 1:c, :]
            attn = jnp.where(tril, qk[h] * ld, 0.0).astype(BF16)
            kdec_t = (k * jnp.exp(gl - cb)).T.astype(BF16)
            lhs_s = jnp.concatenate([xs[h][:, hd:].astype(BF16),
                                     q_ref[0, r0:r0 + c, lanes] * egc[h]], axis=0)
            per_head.append((xs[h][:, :hd], lhs_s, jnp.concatenate([attn, kdec_t], axis=0),
                             jnp.exp(gl)))
        return per_head

    state = [s_ref[h] for h in heads]
    cur = stage_a(0)
    for n in range(n_chunks):
        r0 = n * c
        ws = [_dot(cur[h][1], state[h].astype(BF16)) for h in heads]
        nxt = stage_a(n + 1) if n + 1 < n_chunks else None
        vn = [(cur[h][0] - ws[h][:c]).astype(BF16) for h in heads]
        upd = [_dot(cur[h][2], vn[h]) for h in heads]
        for h in heads:
            lanes = slice(h * hd, (h + 1) * hd)
            o_ref[0, r0:r0 + c, lanes] = (ws[h][c:] + upd[h][:c]).astype(BF16)
            state[h] = state[h] * cur[h][3] + upd[h][c:]
        cur = nxt
    for h in heads:
        s_ref[h] = state[h]


def _deltanet(dq, dk, dv, ab, abt, a_log, dt_bias, rows=256, scan_rows=1024):
    bsz, t, w = dq.shape
    c = DN_CHUNK
    n_chunks = rows // c
    total_chunks = t // c
    abt_c = abt.reshape(bsz, 2 * DN_HEADS, total_chunks, c).transpose(0, 2, 1, 3)
    pad = jnp.zeros((DN_HEADS,), F32)
    alog16 = jnp.concatenate([a_log, pad])
    dtb16 = jnp.concatenate([dt_bias, pad])
    small = [alog16.reshape(1, -1), dtb16.reshape(1, -1), alog16.reshape(-1, 1), dtb16.reshape(-1, 1)]
    small_specs = [_const_spec((1, 2 * DN_HEADS), single_buffer=False)] * 2 + \
                  [_const_spec((2 * DN_HEADS, 1), single_buffer=False)] * 2
    row = pl.BlockSpec((1, rows, w), lambda b, i: (b, i, 0))
    ab_spec = pl.BlockSpec((1, rows, 2 * DN_HEADS), lambda b, i: (b, i, 0))
    abt_spec = pl.BlockSpec((1, n_chunks, 2 * DN_HEADS, c), lambda b, i: (b, i, 0, 0))
    mat_spec = pl.BlockSpec((1, n_chunks, DN_HEADS, c, c), lambda b, i: (b, i, 0, 0, 0))
    mat_shape = jax.ShapeDtypeStruct((bsz, total_chunks, DN_HEADS, c, c), F32)
    params = pltpu.CompilerParams(dimension_semantics=("arbitrary", "arbitrary"),
                                  vmem_limit_bytes=VMEM_LIMIT)

    a_mat, gcc, gcr = pl.pallas_call(
        functools.partial(_dn_prep_kernel, n_chunks=n_chunks),
        grid=(bsz, t // rows),
        in_specs=[row, ab_spec, abt_spec] + small_specs,
        out_specs=[mat_spec, ab_spec, abt_spec],
        out_shape=[mat_shape, jax.ShapeDtypeStruct(ab.shape, F32),
                   jax.ShapeDtypeStruct(abt_c.shape, F32)],
        compiler_params=params,
        name="dn_prep",
    )(dk, ab, abt_c, *small)

    n_sys = bsz * total_chunks * DN_HEADS
    assert n_sys % LANE == 0
    flat_spec = pl.BlockSpec((LANE, c * c), lambda i: (i, 0))
    t_mat = pl.pallas_call(
        _dn_solve_kernel,
        grid=(n_sys // LANE,),
        in_specs=[flat_spec],
        out_specs=flat_spec,
        out_shape=jax.ShapeDtypeStruct((n_sys, c * c), BF16),
        scratch_shapes=[pltpu.VMEM((c, c, LANE), F32), pltpu.VMEM((c, c, LANE), F32)],
        compiler_params=pltpu.CompilerParams(dimension_semantics=("arbitrary",),
                                             vmem_limit_bytes=VMEM_LIMIT),
        name="dn_solve",
    )(a_mat.reshape(n_sys, c * c)).reshape(mat_shape.shape)

    sn = scan_rows // c
    srow = pl.BlockSpec((1, scan_rows, w), lambda b, i: (b, i, 0))
    sab_spec = pl.BlockSpec((1, scan_rows, 2 * DN_HEADS), lambda b, i: (b, i, 0))
    sabt_spec = pl.BlockSpec((1, sn, 2 * DN_HEADS, c), lambda b, i: (b, i, 0, 0))
    smat_spec = pl.BlockSpec((1, sn, DN_HEADS, c, c), lambda b, i: (b, i, 0, 0, 0))
    return pl.pallas_call(
        functools.partial(_dn_scan_kernel, n_chunks=sn),
        grid=(bsz, t // scan_rows),
        in_specs=[srow, srow, srow, sab_spec, sabt_spec, smat_spec],
        out_specs=srow,
        out_shape=jax.ShapeDtypeStruct((bsz, t, w), BF16),
        scratch_shapes=[pltpu.VMEM((DN_HEADS, DN_HEAD_DIM, DN_HEAD_DIM), F32)],
        compiler_params=params,
        name="dn_scan",
    )(dq, dk, dv, gcc, gcr, t_mat)


def _merge_kernel(o1_ref, l1_ref, o2_ref, l2_ref, o3_ref, l3_ref, odn_ref, gate_ref, nrm_ref,
                  mg_ref, x_ref, gt_ref, wpa_ref, wpd_ref, wo_ref, out_ref, nat_scr):
    d = D_MODEL
    tm = x_ref.shape[1]
    slabs = ATT_GROUP_W // LANE
    for a, (ref, dil) in enumerate(((o2_ref, DILATIONS[1]), (l2_ref, DILATIONS[1]),
                                    (o3_ref, DILATIONS[2]), (l3_ref, DILATIONS[2]))):
        for r in range(dil):
            for s in range(slabs):
                lo = r * ATT_GROUP_W + s * LANE
                nat_scr[a, s, pl.ds(r, tm // dil, stride=dil), :] = ref[0, :, lo:lo + LANE]
    halves = [slice(i * (tm // 2), (i + 1) * (tm // 2)) for i in range(2)]
    yd = []
    for r in halves:
        gated = []
        for h in range(DN_HEADS):
            lanes = slice(h * DN_HEAD_DIM, (h + 1) * DN_HEAD_DIM)
            o = odn_ref[0, r, lanes].astype(F32)
            ms = jnp.mean(o * o, axis=-1, keepdims=True)
            on = (o * lax.rsqrt(ms + EPS)) * nrm_ref[...]
            gated.append((on * _silu(gate_ref[0, r, lanes].astype(F32))).astype(BF16))
        yd.append(_dot(jnp.concatenate(gated, axis=1), wpd_ref[...]))
    ya = []
    for r in halves:
        ys = []
        for s in range(slabs):
            cols = slice(s * LANE, (s + 1) * LANE)
            l1, l2, l3 = l1_ref[0, r, cols], nat_scr[1, s, r, :], nat_scr[3, s, r, :]
            m = jnp.maximum(jnp.maximum(l1, l2), l3)
            e1, e2, e3 = jnp.exp(l1 - m), jnp.exp(l2 - m), jnp.exp(l3 - m)
            y = e1 * o1_ref[0, r, cols] + e2 * nat_scr[0, s, r, :] + e3 * nat_scr[2, s, r, :]
            ys.append((y / (e1 + e2 + e3)).astype(BF16))
        ya.append(_dot(jnp.concatenate(ys, axis=1), wpa_ref[...]))
    for i, r in enumerate(halves):
        mix = mg_ref[0, r, :d].astype(F32) * ya[i] + mg_ref[0, r, d:].astype(F32) * yd[i]
        out = _dot(mix.astype(BF16), wo_ref[...])
        out_ref[0, r] = x_ref[0, r] + gt_ref[0] * out


def _merge(att, odn, gate, dn_norm, mg, x, mod_l, wpa, wpd, wo, layer, tm=512):
    bsz, t, d = x.shape
    row = lambda n: pl.BlockSpec((1, tm, n), lambda b, i: (b, i, 0))
    view = lambda dil: pl.BlockSpec((1, tm // dil, dil * ATT_GROUP_W), lambda b, i: (b, i, 0))
    att_specs = [row(ATT_GROUP_W)] * 2 + [view(DILATIONS[1])] * 2 + [view(DILATIONS[2])] * 2
    return pl.pallas_call(
        _merge_kernel,
        grid=(bsz, t // tm),
        in_specs=att_specs + [row(DN_W), row(DN_W), _const_spec((1, DN_HEAD_DIM), single_buffer=False),
                              row(2 * d), row(d), _mod_spec(5),
                              _layer_spec(wpa.shape[1:], layer), _layer_spec(wpd.shape[1:], layer),
                              _layer_spec(wo.shape[1:], layer)],
        out_specs=row(d),
        out_shape=jax.ShapeDtypeStruct(x.shape, F32),
        scratch_shapes=[pltpu.VMEM((4, ATT_GROUP_W // LANE, tm, LANE), F32)],
        compiler_params=pltpu.CompilerParams(
            dimension_semantics=("arbitrary", "arbitrary"), vmem_limit_bytes=VMEM_LIMIT),
        name="merge_out",
    )(*att, odn, gate, dn_norm.reshape(1, -1), mg, x, mod_l, wpa, wpd, wo)


def _mixer(x, mod_l, layer, gain, w_packed, ones_blk, qn4, kn4, conv_w, a_log, dt_bias, dn_norm,
           wpa, wpd, wo):
    za0, za1, za2, dq, dk, dv, gate, mg, ab, abt = _inproj(
        x, mod_l, gain, w_packed, layer, ones_blk, qn4, kn4, conv_w)
    att = _attention(za0, za1, za2)
    odn = _deltanet(dq, dk, dv, ab, abt, a_log, dt_bias)
    return _merge(att, odn, gate, dn_norm, mg, x, mod_l, wpa, wpd, wo, layer)


def kernel(x, c, ada_w, ada_b, norm_ff1, ffn1_w_up, ffn1_w_down, norm_mix, w_in, q_norm, k_norm,
           conv_w, a_log, dt_bias, dn_norm, w_proj_att, w_proj_dn, w_out, norm_ff2, ffn2_w_up,
           ffn2_w_down):
    bsz = x.shape[0]
    mod = _ada_mod(c, ada_w, ada_b)
    blk = jnp.arange(ATT_GROUP_W) // ATT_HEAD_DIM
    ones_blk = jnp.where(blk[:, None] == blk[None, :], 1.0 / ATT_HEAD_DIM, 0.0).astype(BF16)
    w_in_packed = _pack_w_in(w_in)
    up1, dn1, up2, dn2 = (w.astype(BF16) for w in (ffn1_w_up, ffn1_w_down, ffn2_w_up, ffn2_w_down))
    wpa, wpd, wo = (w.astype(BF16) for w in (w_proj_att, w_proj_dn, w_out))
    for l in range(DEPTH):
        mod_l = mod[l].reshape(bsz, 1, N_ADA * D_MODEL)
        x = _ffn(x, mod_l, 0, norm_ff1[l], up1, dn1, l)
        x = _mixer(x, mod_l, l, norm_mix[l], w_in_packed, ones_blk,
                   jnp.tile(q_norm[l], ATT_HEADS).reshape(1, -1),
                   jnp.tile(k_norm[l], ATT_HEADS).reshape(1, -1),
                   conv_w[l], a_log[l], dt_bias[l], dn_norm[l], wpa, wpd, wo)
        x = _ffn(x, mod_l, 6, norm_ff2[l], up2, dn2, l)
    return x
```
